```python
import math
import jax, jax.numpy as jnp
from jax import lax
import numpy as np

D_MODEL = 2048
BATCH = 4
SEQ = 8192
DEPTH = 1
DEC_BATCH = 8
DEC_SEQ = 2048
PAST_LEN = 128

HEAD_DIM = 128
N_HEADS = D_MODEL // (2 * HEAD_DIM)
N_KV_HEADS = 2
Q_PER_KV = N_HEADS // N_KV_HEADS
ATTN_WIDTH = N_HEADS * HEAD_DIM
KV_WIDTH = N_KV_HEADS * HEAD_DIM
FOURIER_WIDTH = D_MODEL - ATTN_WIDTH
N_FOURIER_GROUPS = 4
FOURIER_GROUP_DIM = FOURIER_WIDTH // N_FOURIER_GROUPS
MIX_WIDTH = ATTN_WIDTH + FOURIER_WIDTH
IN_WIDTH = ATTN_WIDTH + 2 * KV_WIDTH + FOURIER_WIDTH
GRID_W = 64
ROPE_THETA = 10000.0
ROPE_AXIS_DIM = HEAD_DIM // 2
ROPE_FREQS = ROPE_AXIS_DIM // 2
Q_BLOCK = 128
N_EXPERT_GROUPS = 4
EXPERTS_PER_GROUP = 4
N_EXPERTS = N_EXPERT_GROUPS * EXPERTS_PER_GROUP
TOP_K_INNER = 2
EXPERT_FF = D_MODEL // 2
EPS = 1e-6

kernel_name = "hymba_attn_fnet_hier_moe_encoder"


def rms_norm(x, g):
    xf = x.astype(jnp.float32)
    y = xf * lax.rsqrt(jnp.mean(xf * xf, axis=-1, keepdims=True) + EPS) * g.astype(jnp.float32)
    return y.astype(x.dtype)


def axial_rope_tables(n):
    rows = n // GRID_W
    row_id = jnp.repeat(jnp.arange(rows, dtype=jnp.float32), GRID_W)
    col_id = jnp.tile(jnp.arange(GRID_W, dtype=jnp.float32), rows)
    inv_freq = jnp.exp(-math.log(ROPE_THETA) * 2.0 * jnp.arange(ROPE_FREQS, dtype=jnp.float32) / ROPE_AXIS_DIM)
    ang = jnp.stack([row_id[:, None] * inv_freq, col_id[:, None] * inv_freq], axis=1)
    return jnp.cos(ang), jnp.sin(ang)


def apply_axial_rope(x, cos, sin):
    xs = x.reshape(*x.shape[:-1], 2, 2, ROPE_FREQS)
    x1, x2 = xs[..., 0, :], xs[..., 1, :]
    c, s = cos[:, None], sin[:, None]
    out = jnp.stack([x1 * c - x2 * s, x2 * c + x1 * s], axis=-2)
    return out.reshape(x.shape)


def block_attention(q, k, v):
    B, n = q.shape[0], q.shape[1]
    nb = n // Q_BLOCK
    scale = 1.0 / math.sqrt(HEAD_DIM)
    qb = q.reshape(B, nb, Q_BLOCK, N_KV_HEADS, Q_PER_KV, HEAD_DIM).transpose(1, 0, 2, 3, 4, 5)

    def one_block(qblk):
        s = jnp.einsum('bqkgd,bskd->bkgqs', qblk, k, preferred_element_type=jnp.float32) * scale
        p = jax.nn.softmax(s, axis=-1)
        return jnp.einsum('bkgqs,bskd->bqkgd', p.astype(v.dtype), v)

    o = lax.map(one_block, qb)
    return o.transpose(1, 0, 2, 3, 4, 5).reshape(B, n, ATTN_WIDTH)


def fourier_mix(f):
    B, n, _ = f.shape
    fg = f.astype(jnp.float32).reshape(B, n, N_FOURIER_GROUPS, FOURIER_GROUP_DIM)
    out = jnp.fft.fft2(fg, axes=(1, 3), norm="ortho").real
    return out.reshape(B, n, FOURIER_WIDTH).astype(f.dtype)


def parallel_mixer(h, w_in, q_norm_g, k_norm_g, w_fmix, attn_out_g, fourier_out_g, w_out):
    B, n, _ = h.shape
    dt = h.dtype
    proj = jnp.einsum('bnd,de->bne', h, w_in)
    q = proj[..., :ATTN_WIDTH].reshape(B, n, N_HEADS, HEAD_DIM)
    k = proj[..., ATTN_WIDTH:ATTN_WIDTH + KV_WIDTH].reshape(B, n, N_KV_HEADS, HEAD_DIM)
    v = proj[..., ATTN_WIDTH + KV_WIDTH:ATTN_WIDTH + 2 * KV_WIDTH].reshape(B, n, N_KV_HEADS, HEAD_DIM)
    f = proj[..., ATTN_WIDTH + 2 * KV_WIDTH:]
    cos, sin = axial_rope_tables(n)
    q = apply_axial_rope(rms_norm(q, q_norm_g).astype(jnp.float32), cos, sin).astype(dt)
    k = apply_axial_rope(rms_norm(k, k_norm_g).astype(jnp.float32), cos, sin).astype(dt)
    attn = block_attention(q.reshape(B, n, N_KV_HEADS, Q_PER_KV, HEAD_DIM), k, v)
    four = jnp.einsum('bnc,ce->bne', fourier_mix(f), w_fmix)
    merged = jnp.concatenate([rms_norm(attn, attn_out_g), rms_norm(four, fourier_out_g)], axis=-1)
    return jnp.einsum('bnm,md->bnd', merged, w_out)


def hierarchical_moe(h, w_router_group, b_router_group, w_router_expert, b_router_expert, w_gate, w_up, w_down):
    B, n, D = h.shape
    t = h.reshape(B * n, D)
    T = t.shape[0]
    g_logits = jnp.einsum('td,dg->tg', t, w_router_group, preferred_element_type=jnp.float32) + b_router_group.astype(jnp.float32)
    g_w, g_idx = lax.top_k(jax.nn.softmax(g_logits, axis=-1), 1)
    e_logits = (jnp.einsum('td,de->te', t, w_router_expert, preferred_element_type=jnp.float32)
                + b_router_expert.astype(jnp.float32)).reshape(T, N_EXPERT_GROUPS, EXPERTS_PER_GROUP)
    e_sel = e_logits[jnp.arange(T), g_idx[:, 0]]
    e_w, e_idx = lax.top_k(jax.nn.softmax(e_sel, axis=-1), TOP_K_INNER)
    e_w = e_w / jnp.sum(e_w, axis=-1, keepdims=True)
    weights = g_w * e_w
    expert_id = g_idx * EXPERTS_PER_GROUP + e_idx
    gate = jnp.sum(jax.nn.one_hot(expert_id, N_EXPERTS, dtype=jnp.float32) * weights[..., None], axis=1)
    y = jnp.zeros((T, D), jnp.float32)
    for e in range(N_EXPERTS):
        a = t @ w_gate[e]
        u = t @ w_up[e]
        y = y + gate[:, e:e + 1] * ((jax.nn.silu(a) * u) @ w_down[e]).astype(jnp.float32)
    return y.reshape(B, n, D).astype(h.dtype)


def encoder_layer(x, norm1_g, w_in, q_norm_g, k_norm_g, w_fmix, attn_out_g, fourier_out_g, w_out,
                  norm2_g, w_router_group, b_router_group, w_router_expert, b_router_expert,
                  w_gate, w_up, w_down):
    x = x + parallel_mixer(rms_norm(x, norm1_g), w_in, q_norm_g, k_norm_g, w_fmix,
                           attn_out_g, fourier_out_g, w_out)
    x = x + hierarchical_moe(rms_norm(x, norm2_g), w_router_group, b_router_group,
                             w_router_expert, b_router_expert, w_gate, w_up, w_down)
    return x


def trunk(x, params):
    for l in range(DEPTH):
        x = encoder_layer(x, *[p[l] for p in params])
    return x


def setup_inputs(seed: int = 0) -> dict:
    key = jax.random.key(seed)
    ks = jax.random.split(key, 20)
    nrm = lambda k, shape, fan: jax.random.normal(k, shape, jnp.float32) * (fan ** -0.5)
    gain = lambda k, shape: 1.0 + 0.02 * jax.random.normal(k, shape, jnp.float32)
    return {
        "x_prompt": jax.random.normal(ks[0], (BATCH, SEQ, D_MODEL), jnp.float32),
        "x_sample": jax.random.normal(ks[1], (DEC_BATCH, DEC_SEQ, D_MODEL), jnp.float32),
        "norm1_g": gain(ks[2], (DEPTH, D_MODEL)),
        "w_in": nrm(ks[3], (DEPTH, D_MODEL, IN_WIDTH), D_MODEL),
        "q_norm_g": gain(ks[4], (DEPTH, HEAD_DIM)),
        "k_norm_g": gain(ks[5], (DEPTH, HEAD_DIM)),
        "w_fmix": nrm(ks[6], (DEPTH, FOURIER_WIDTH, FOURIER_WIDTH), FOURIER_WIDTH),
        "attn_out_g": gain(ks[7], (DEPTH, ATTN_WIDTH)),
        "fourier_out_g": gain(ks[8], (DEPTH, FOURIER_WIDTH)),
        "w_out": nrm(ks[9], (DEPTH, MIX_WIDTH, D_MODEL), MIX_WIDTH),
        "norm2_g": gain(ks[10], (DEPTH, D_MODEL)),
        "w_router_group": nrm(ks[11], (DEPTH, D_MODEL, N_EXPERT_GROUPS), D_MODEL),
        "b_router_group": 0.01 * jax.random.normal(ks[12], (DEPTH, N_EXPERT_GROUPS), jnp.float32),
        "w_router_expert": nrm(ks[13], (DEPTH, D_MODEL, N_EXPERTS), D_MODEL),
        "b_router_expert": 0.01 * jax.random.normal(ks[14], (DEPTH, N_EXPERTS), jnp.float32),
        "w_gate": nrm(ks[15], (DEPTH, N_EXPERTS, D_MODEL, EXPERT_FF), D_MODEL),
        "w_up": nrm(ks[16], (DEPTH, N_EXPERTS, D_MODEL, EXPERT_FF), D_MODEL),
        "w_down": nrm(ks[17], (DEPTH, N_EXPERTS, EXPERT_FF, D_MODEL), EXPERT_FF),
    }


def reference(x_prompt, x_sample, norm1_g, w_in, q_norm_g, k_norm_g, w_fmix, attn_out_g, fourier_out_g,
              w_out, norm2_g, w_router_group, b_router_group, w_router_expert, b_router_expert,
              w_gate, w_up, w_down):
    params = (norm1_g, w_in, q_norm_g, k_norm_g, w_fmix, attn_out_g, fourier_out_g, w_out,
              norm2_g, w_router_group, b_router_group, w_router_expert, b_router_expert,
              w_gate, w_up, w_down)
    y_prompt = trunk(x_prompt, params)
    y_sample = trunk(x_sample, params)
    return (y_prompt, y_sample)
```

```python
import functools
import math

import numpy as np
import jax
import jax.numpy as jnp
from jax import lax
from jax.experimental import pallas as pl
from jax.experimental.pallas import tpu as pltpu

F32 = jnp.float32
BF16 = jnp.bfloat16

EPS = 1e-6
D_MODEL = 2048
HEAD_DIM = 128
N_HEADS = 8
N_KV_HEADS = 2
Q_PER_KV = N_HEADS // N_KV_HEADS
ATTN_WIDTH = N_HEADS * HEAD_DIM
KV_WIDTH = N_KV_HEADS * HEAD_DIM
FOURIER_WIDTH = D_MODEL - ATTN_WIDTH
N_FOURIER_GROUPS = 4
FOURIER_GROUP_DIM = FOURIER_WIDTH // N_FOURIER_GROUPS
IN_WIDTH = ATTN_WIDTH + 2 * KV_WIDTH + FOURIER_WIDTH
GRID_W = 64
ROPE_THETA = 10000.0
ROPE_FREQS = HEAD_DIM // 4
N_EXPERT_GROUPS = 4
EXPERTS_PER_GROUP = 4
N_EXPERTS = N_EXPERT_GROUPS * EXPERTS_PER_GROUP
EXPERT_FF = D_MODEL // 2
N_PAIRS = 6
N_BUCKETS = N_EXPERT_GROUPS * N_PAIRS

LANES = 128
TOKEN_TILE = 512
ATTN_TILE = 1024
EXPERT_TILE = 256
DFT_N2 = 32
DFT3_COLS = 8192
VMEM_LIMIT = 56 * 1024 * 1024
NEG_BIG = -1e30
LOG2E = 1.4426950408889634


def _cparams(sem, vmem=VMEM_LIMIT):
    return pltpu.CompilerParams(dimension_semantics=sem, vmem_limit_bytes=vmem)


def _const_spec(shape):
    nd = len(shape)
    return pl.BlockSpec(shape, lambda *_: (0,) * nd, pipeline_mode=pl.Buffered(1))


def _inproj_kernel(x_ref, g1_ref, w_ref, qg_ref, kg_ref, c_ref, s1_ref, s2_ref,
                   q_ref, k_ref, vt_ref, f_ref, *, q_scale):
    x = x_ref[...]
    ms = jnp.mean(x * x, axis=-1, keepdims=True)
    h = (x * lax.rsqrt(ms + EPS) * g1_ref[...]).astype(BF16)
    c = c_ref[...]
    s1 = s1_ref[...]
    s2 = s2_ref[...]

    def norm_rope(z, g):
        zms = jnp.mean(z * z, axis=-1, keepdims=True)
        zn = z * lax.rsqrt(zms + EPS) * g
        return zn * c + pltpu.roll(zn, 96, 1) * s1 + pltpu.roll(zn, 32, 1) * s2

    qg = qg_ref[...]
    kg = kg_ref[...]
    pq = jnp.dot(h, w_ref[:, 0:ATTN_WIDTH], preferred_element_type=F32)
    for hh in range(N_HEADS):
        sl = slice(hh * HEAD_DIM, (hh + 1) * HEAD_DIM)
        q_ref[:, sl] = (norm_rope(pq[:, sl], qg) * q_scale).astype(BF16)
    pk = jnp.dot(h, w_ref[:, ATTN_WIDTH:ATTN_WIDTH + KV_WIDTH], preferred_element_type=F32)
    for hh in range(N_KV_HEADS):
        sl = slice(hh * HEAD_DIM, (hh + 1) * HEAD_DIM)
        k_ref[:, sl] = norm_rope(pk[:, sl], kg).astype(BF16)
    pv = jnp.dot(h, w_ref[:, ATTN_WIDTH + KV_WIDTH:ATTN_WIDTH + 2 * KV_WIDTH],
                 preferred_element_type=F32)
    vt_ref[0] = pv.T.astype(BF16)
    pf = jnp.dot(h, w_ref[:, ATTN_WIDTH + 2 * KV_WIDTH:IN_WIDTH], preferred_element_type=F32)
    f_ref[...] = pf.astype(BF16)


def _in_proj(x2d, n, g1, w_in, qg, kg, rope_c, rope_s1, rope_s2):
    t = x2d.shape[0]
    tm = TOKEN_TILE
    nt = n // tm
    batch = t // n
    q_scale = LOG2E / math.sqrt(HEAD_DIM)
    return pl.pallas_call(
        functools.partial(_inproj_kernel, q_scale=q_scale),
        grid=(t // tm,),
        in_specs=[
            pl.BlockSpec((tm, D_MODEL), lambda i: (i, 0)),
            _const_spec((1, D_MODEL)),
            _const_spec((D_MODEL, IN_WIDTH)),
            _const_spec((1, HEAD_DIM)),
            _const_spec((1, HEAD_DIM)),
            pl.BlockSpec((tm, HEAD_DIM), lambda i: (i % nt, 0)),
            pl.BlockSpec((tm, HEAD_DIM), lambda i: (i % nt, 0)),
            pl.BlockSpec((tm, HEAD_DIM), lambda i: (i % nt, 0)),
        ],
        out_specs=[
            pl.BlockSpec((tm, ATTN_WIDTH), lambda i: (i, 0)),
            pl.BlockSpec((tm, KV_WIDTH), lambda i: (i, 0)),
            pl.BlockSpec((1, KV_WIDTH, tm), lambda i: (i // nt, 0, i % nt)),
            pl.BlockSpec((tm, FOURIER_WIDTH), lambda i: (i, 0)),
        ],
        out_shape=[
            jax.ShapeDtypeStruct((t, ATTN_WIDTH), BF16),
            jax.ShapeDtypeStruct((t, KV_WIDTH), BF16),
            jax.ShapeDtypeStruct((batch, KV_WIDTH, n), BF16),
            jax.ShapeDtypeStruct((t, FOURIER_WIDTH), BF16),
        ],
        compiler_params=_cparams(("arbitrary",)),
        name="in_proj",
    )(x2d, g1, w_in, qg, kg, rope_c, rope_s1, rope_s2)


def _attn_kernel(q_ref, k_ref, vt_ref, o_ref, m_ref, l_ref, acc_ref, *, nk):
    ki = pl.program_id(3)

    @pl.when(ki == 0)
    def _():
        m_ref[...] = jnp.full(m_ref.shape, NEG_BIG, F32)
        l_ref[...] = jnp.zeros(l_ref.shape, F32)
        acc_ref[...] = jnp.zeros(acc_ref.shape, F32)

    k = k_ref[0]
    vt = vt_ref[0]
    for g in range(Q_PER_KV):
        q = q_ref[0, :, g * HEAD_DIM:(g + 1) * HEAD_DIM]
        s = lax.dot_general(k, q, (((1,), (1,)), ((), ())), preferred_element_type=F32)
        m_prev = m_ref[g]
        m_new = jnp.maximum(m_prev, jnp.max(s, axis=0, keepdims=True))
        alpha = jnp.exp2(m_prev - m_new)
        p = jnp.exp2(s - m_new)
        l_ref[g] = alpha * l_ref[g] + jnp.sum(p, axis=0, keepdims=True)
        acc_ref[g] = alpha * acc_ref[g] + jnp.dot(vt, p.astype(BF16), preferred_element_type=F32)
        m_ref[g] = m_new

    @pl.when(ki == nk - 1)
    def _():
        for g in range(Q_PER_KV):
            o = acc_ref[g] * (1.0 / l_ref[g])
            o_ref[0, :, g * HEAD_DIM:(g + 1) * HEAD_DIM] = o.T.astype(BF16)


def _attention(q, k, vt):
    batch, n, _ = q.shape
    tq = tk = min(ATTN_TILE, n)
    gw = Q_PER_KV * HEAD_DIM
    return pl.pallas_call(
        functools.partial(_attn_kernel, nk=n // tk),
        grid=(batch, N_KV_HEADS, n // tq, n // tk),
        in_specs=[
            pl.BlockSpec((1, tq, gw), lambda b, h, qi, ki: (b, qi, h)),
            pl.BlockSpec((1, tk, HEAD_DIM), lambda b, h, qi, ki: (b, ki, h)),
            pl.BlockSpec((1, HEAD_DIM, tk), lambda b, h, qi, ki: (b, h, ki)),
        ],
        out_specs=pl.BlockSpec((1, tq, gw), lambda b, h, qi, ki: (b, qi, h)),
        out_shape=jax.ShapeDtypeStruct((batch, n, ATTN_WIDTH), BF16),
        scratch_shapes=[
            pltpu.VMEM((Q_PER_KV, 1, tq), F32),
            pltpu.VMEM((Q_PER_KV, 1, tq), F32),
            pltpu.VMEM((Q_PER_KV, HEAD_DIM, tq), F32),
        ],
        compiler_params=_cparams(("arbitrary", "arbitrary", "arbitrary", "arbitrary")),
        name="attention",
    )(q, k, vt)


def _dft1_kernel(f_ref, fc_ref, twc_ref, tws_ref, br_ref, bi_ref, *, n1):
    x = f_ref[0]
    a = jnp.dot(fc_ref[...].astype(BF16), x, preferred_element_type=F32)
    ar = a[:n1]
    ai = a[n1:]
    c = twc_ref[0]
    s = tws_ref[0]
    for j in range(x.shape[1] // LANES):
        sl = slice(j * LANES, (j + 1) * LANES)
        br_ref[0, 0, :, sl] = (ar[:, sl] * c + ai[:, sl] * s).astype(BF16)
        bi_ref[0, 0, :, sl] = (ai[:, sl] * c - ar[:, sl] * s).astype(BF16)


def _dft3_kernel(br_ref, bi_ref, c3_ref, s3_ref, u_ref, v_ref):
    br = br_ref[0]
    bi = bi_ref[0]
    c3 = c3_ref[...].astype(BF16)
    s3 = s3_ref[...].astype(BF16)
    u = jnp.dot(c3, br, preferred_element_type=F32) + jnp.dot(s3, bi, preferred_element_type=F32)
    v = jnp.dot(c3, bi, preferred_element_type=F32) - jnp.dot(s3, br, preferred_element_type=F32)
    u_ref[0] = u.astype(BF16)
    v_ref[0] = v.astype(BF16)


def _dft_tables(n):
    n2 = DFT_N2
    n1 = n // n2
    scale = 1.0 / math.sqrt(n * FOURIER_GROUP_DIM)
    k1 = np.arange(n1)[:, None]
    a1 = 2.0 * np.pi * ((k1 * np.arange(n1)[None, :]) % n1) / n1
    fcat = np.concatenate([np.cos(a1), -np.sin(a1)], axis=0) * scale
    th = 2.0 * np.pi * ((np.arange(n2)[:, None] * k1.T) % n) / n
    twc = np.repeat(np.cos(th)[:, :, None], LANES, axis=2)
    tws = np.repeat(np.sin(th)[:, :, None], LANES, axis=2)
    a3 = 2.0 * np.pi * ((np.arange(n2)[:, None] * np.arange(n2)[None, :]) % n2) / n2
    return (jnp.asarray(fcat, F32), jnp.asarray(twc, F32), jnp.asarray(tws, F32),
            jnp.asarray(np.cos(a3), F32), jnp.asarray(np.sin(a3), F32))


def _position_dft(f, n):
    batch, _, c = f.shape
    n2 = DFT_N2
    n1 = n // n2
    fcat, twc, tws, c3, s3 = _dft_tables(n)
    f2 = f.reshape(batch, n1, n2 * c)
    br, bi = pl.pallas_call(
        functools.partial(_dft1_kernel, n1=n1),
        grid=(batch, n2),
        in_specs=[
            pl.BlockSpec((1, n1, c), lambda b, j: (b, 0, j)),
            _const_spec((2 * n1, n1)),
            pl.BlockSpec((1, n1, LANES), lambda b, j: (j, 0, 0)),
            pl.BlockSpec((1, n1, LANES), lambda b, j: (j, 0, 0)),
        ],
        out_specs=[
            pl.BlockSpec((1, 1, n1, c), lambda b, j: (b, j, 0, 0)),
            pl.BlockSpec((1, 1, n1, c), lambda b, j: (b, j, 0, 0)),
        ],
        out_shape=[jax.ShapeDtypeStruct((batch, n2, n1, c), BF16)] * 2,
        compiler_params=_cparams(("arbitrary", "arbitrary")),
        name="dft_stage1",
    )(f2, fcat, twc, tws)
    br2 = br.reshape(batch, n2, n1 * c)
    bi2 = bi.reshape(batch, n2, n1 * c)
    w = min(DFT3_COLS, n1 * c)
    u, v = pl.pallas_call(
        _dft3_kernel,
        grid=(batch, (n1 * c) // w),
        in_specs=[
            pl.BlockSpec((1, n2, w), lambda b, j: (b, 0, j)),
            pl.BlockSpec((1, n2, w), lambda b, j: (b, 0, j)),
            _const_spec((n2, n2)),
            _const_spec((n2, n2)),
        ],
        out_specs=[
            pl.BlockSpec((1, n2, w), lambda b, j: (b, 0, j)),
            pl.BlockSpec((1, n2, w), lambda b, j: (b, 0, j)),
        ],
        out_shape=[jax.ShapeDtypeStruct((batch, n2, n1 * c), BF16)] * 2,
        compiler_params=_cparams(("arbitrary", "arbitrary")),
        name="dft_stage3",
    )(br2, bi2, c3, s3)
    return u.reshape(batch * n, c), v.reshape(batch * n, c)


def _rms(z, g):
    ms = jnp.mean(z * z, axis=-1, keepdims=True)
    return z * lax.rsqrt(ms + EPS) * g


def _outproj_kernel(x_ref, a_ref, u_ref, v_ref, cc_ref, sc_ref, wf_ref, ga_ref, gf_ref, wo_ref,
                    g2_ref, wr_ref, br_ref,
                    x1_ref, h2_ref, info_ref, cnt_ref):
    i = pl.program_id(0)
    cc = cc_ref[...].astype(BF16)
    sc = sc_ref[...].astype(BF16)
    parts = []
    for g in range(N_FOURIER_GROUPS):
        sl = slice(g * FOURIER_GROUP_DIM, (g + 1) * FOURIER_GROUP_DIM)
        parts.append(jnp.dot(u_ref[:, sl], cc, preferred_element_type=F32)
                     + jnp.dot(v_ref[:, sl], sc, preferred_element_type=F32))
    fmix = jnp.concatenate(parts, axis=-1).astype(BF16)
    four = jnp.dot(fmix, wf_ref[...], preferred_element_type=F32)
    fn = _rms(four, gf_ref[...]).astype(BF16)
    an = _rms(a_ref[...].astype(F32), ga_ref[...]).astype(BF16)
    mix = (jnp.dot(an, wo_ref[0:ATTN_WIDTH, :], preferred_element_type=F32)
           + jnp.dot(fn, wo_ref[ATTN_WIDTH:D_MODEL, :], preferred_element_type=F32))
    x1 = x_ref[...] + mix
    x1_ref[...] = x1
    h2 = _rms(x1, g2_ref[...]).astype(BF16)
    h2_ref[...] = h2

    lg = jnp.dot(h2, wr_ref[...], preferred_element_type=F32) + br_ref[...]
    lane = lax.broadcasted_iota(jnp.int32, lg.shape, 1).astype(F32)
    is_g = lane < N_EXPERT_GROUPS
    gl = jnp.where(is_g, lg, NEG_BIG)
    gmax = jnp.max(gl, axis=-1, keepdims=True)
    gidx = jnp.min(jnp.where(gl == gmax, lane, float(LANES)), axis=-1, keepdims=True)
    gsum = jnp.sum(jnp.where(is_g, jnp.exp(gl - gmax), 0.0), axis=-1, keepdims=True)
    g_w = 1.0 / gsum
    lo_lane = N_EXPERT_GROUPS + EXPERTS_PER_GROUP * gidx
    in_grp = jnp.logical_and(lane >= lo_lane, lane < lo_lane + EXPERTS_PER_GROUP)
    el = jnp.where(in_grp, lg, NEG_BIG)
    m1 = jnp.max(el, axis=-1, keepdims=True)
    i1 = jnp.min(jnp.where(el == m1, lane, float(LANES)), axis=-1, keepdims=True)
    el2 = jnp.where(lane == i1, NEG_BIG, el)
    m2 = jnp.max(el2, axis=-1, keepdims=True)
    i2 = jnp.min(jnp.where(el2 == m2, lane, float(LANES)), axis=-1, keepdims=True)
    r = jnp.exp(m2 - m1)
    w1 = g_w / (1.0 + r)
    w2 = g_w * r / (1.0 + r)
    first_lo = i1 < i2
    ea = jnp.minimum(i1, i2) - lo_lane
    eb = jnp.maximum(i1, i2) - lo_lane
    pair = ea * (7.0 - ea) * 0.5 + (eb - ea - 1.0)
    bucket = gidx * float(N_PAIRS) + pair
    w_lo = jnp.where(first_lo, w1, w2)
    w_hi = jnp.where(first_lo, w2, w1)
    info_ref[...] = jnp.where(lane == 0.0, bucket,
                              jnp.where(lane == 1.0, w_lo, jnp.where(lane == 2.0, w_hi, 0.0)))
    cnt = jnp.sum(jnp.where(lane == bucket, 1.0, 0.0), axis=0, keepdims=True)

    @pl.when(i == 0)
    def _():
        cnt_ref[...] = jnp.zeros(cnt_ref.shape, F32)

    cnt_ref[...] = cnt_ref[...] + cnt


def _out_proj(x2d, attn, u, v, cc, sc, w_fmix, ga, gf, w_out, g2, w_router, b_router):
    t = x2d.shape[0]
    tm = TOKEN_TILE
    row = lambda i: (i, 0)
    return pl.pallas_call(
        _outproj_kernel,
        grid=(t // tm,),
        in_specs=[
            pl.BlockSpec((tm, D_MODEL), row),
            pl.BlockSpec((tm, ATTN_WIDTH), row),
            pl.BlockSpec((tm, FOURIER_WIDTH), row),
            pl.BlockSpec((tm, FOURIER_WIDTH), row),
            _const_spec((FOURIER_GROUP_DIM, FOURIER_GROUP_DIM)),
            _const_spec((FOURIER_GROUP_DIM, FOURIER_GROUP_DIM)),
            _const_spec((FOURIER_WIDTH, FOURIER_WIDTH)),
            _const_spec((1, ATTN_WIDTH)),
            _const_spec((1, FOURIER_WIDTH)),
            _const_spec((D_MODEL, D_MODEL)),
            _const_spec((1, D_MODEL)),
            _const_spec((D_MODEL, LANES)),
            _const_spec((1, LANES)),
        ],
        out_specs=[
            pl.BlockSpec((tm, D_MODEL), row),
            pl.BlockSpec((tm, D_MODEL), row),
            pl.BlockSpec((tm, LANES), row),
            pl.BlockSpec((8, LANES), lambda i: (0, 0)),
        ],
        out_shape=[
            jax.ShapeDtypeStruct((t, D_MODEL), F32),
            jax.ShapeDtypeStruct((t, D_MODEL), BF16),
            jax.ShapeDtypeStruct((t, LANES), F32),
            jax.ShapeDtypeStruct((8, LANES), F32),
        ],
        compiler_params=_cparams(("arbitrary",)),
        name="out_proj_router",
    )(x2d, attn, u, v, cc, sc, w_fmix, ga, gf, w_out, g2, w_router, b_router)


def _rank_kernel(info_ref, start_ref, pos_ref, base_ref):
    i = pl.program_id(0)
    tm = info_ref.shape[0]

    @pl.when(i == 0)
    def _():
        base_ref[...] = start_ref[...]

    bucket = info_ref[...].T[0:1, :]
    brow = lax.broadcasted_iota(jnp.int32, (32, tm), 0).astype(F32)
    onehot = jnp.where(brow == bucket, 1.0, 0.0)
    oh = onehot.astype(BF16)
    src = lax.broadcasted_iota(jnp.int32, (tm, tm), 0)
    dst = lax.broadcasted_iota(jnp.int32, (tm, tm), 1)
    upper = jnp.where(src <= dst, 1.0, 0.0).astype(BF16)
    cum = jnp.dot(oh, upper, preferred_element_type=F32)
    base = base_ref[...]
    base_t = jnp.concatenate([base] * (tm // LANES), axis=1)
    pos = jnp.sum(onehot * (cum - 1.0 + base_t), axis=0, keepdims=True)
    pos_ref[0] = pos.astype(jnp.int32)
    ones = jnp.ones((tm, LANES), BF16)
    base_ref[...] = base + jnp.dot(oh, ones, preferred_element_type=F32)


def _rank(info, starts):
    t = info.shape[0]
    tm = TOKEN_TILE
    return pl.pallas_call(
        _rank_kernel,
        grid=(t // tm,),
        in_specs=[
            pl.BlockSpec((tm, LANES), lambda i: (i, 0)),
            pl.BlockSpec((32, LANES), lambda i: (0, 0)),
        ],
        out_specs=pl.BlockSpec((1, 1, tm), lambda i: (i, 0, 0)),
        out_shape=jax.ShapeDtypeStruct((t // tm, 1, tm), jnp.int32),
        scratch_shapes=[pltpu.VMEM((32, LANES), F32)],
        compiler_params=_cparams(("arbitrary",)),
        name="rank",
    )(info, starts)


def _scatter_kernel(pos_ref, h_ref, xs_in_ref, xs_ref, sem):
    del xs_in_ref
    tm = h_ref.shape[0]

    def row_copy(r):
        return pltpu.make_async_copy(h_ref.at[r], xs_ref.at[pos_ref[0, 0, r]], sem)

    def start(r, carry):
        row_copy(r).start()
        return carry

    lax.fori_loop(0, tm, start, 0)

    def wait(r, carry):
        row_copy(r).wait()
        return carry

    lax.fori_loop(0, tm, wait, 0)


def _scatter(pos, h2, xs):
    t = h2.shape[0]
    tm = TOKEN_TILE
    sub = D_MODEL // LANES
    h3 = h2.reshape(t, sub, LANES)
    return pl.pallas_call(
        _scatter_kernel,
        grid=(t // tm,),
        in_specs=[
            pl.BlockSpec((1, 1, tm), lambda i: (i, 0, 0), memory_space=pltpu.SMEM),
            pl.BlockSpec((tm, sub, LANES), lambda i: (i, 0, 0)),
            pl.BlockSpec(memory_space=pl.ANY),
        ],
        out_specs=pl.BlockSpec(memory_space=pl.ANY),
        out_shape=jax.ShapeDtypeStruct(xs.shape, xs.dtype),
        scratch_shapes=[pltpu.SemaphoreType.DMA],
        input_output_aliases={2: 0},
        compiler_params=_cparams(("arbitrary",)),
        name="scatter_rows",
    )(pos, h3, xs)


def _expert_kernel(lo_ref, hi_ref, blk_ref, valid_ref, x_ref,
                   wg0_ref, wu0_ref, wd0_ref, wg1_ref, wu1_ref, wd1_ref, o_ref):
    del lo_ref, hi_ref, blk_ref
    i = pl.program_id(0)

    @pl.when(valid_ref[i] == 1)
    def _():
        x = x_ref[...]

        def mlp(wg_ref, wu_ref, wd_ref):
            a = jnp.dot(x, wg_ref[0], preferred_element_type=F32)
            u = jnp.dot(x, wu_ref[0], preferred_element_type=F32)
            h = (a * (1.0 / (1.0 + jnp.exp(-a))) * u).astype(BF16)
            return jnp.dot(h, wd_ref[0], preferred_element_type=F32)

        e_lo = mlp(wg0_ref, wu0_ref, wd0_ref)
        e_hi = mlp(wg1_ref, wu1_ref, wd1_ref)
        o_ref[...] = pltpu.pack_elementwise([e_lo, e_hi], packed_dtype=BF16)

    @pl.when(valid_ref[i] == 0)
    def _():
        zero = jnp.zeros(o_ref.shape, F32)
        o_ref[...] = pltpu.pack_elementwise([zero, zero], packed_dtype=BF16)


def _experts(lo, hi, blk, valid, xs, wg, wu, wd):
    n_tiles = lo.shape[0]
    r = EXPERT_TILE
    x_map = lambda i, lo, hi, blk, valid: (blk[i], 0)
    lo_map = lambda i, lo, hi, blk, valid: (lo[i], 0, 0)
    hi_map = lambda i, lo, hi, blk, valid: (hi[i], 0, 0)
    one = pl.Buffered(1)
    w_up = (1, D_MODEL, EXPERT_FF)
    w_dn = (1, EXPERT_FF, D_MODEL)
    return pl.pallas_call(
        _expert_kernel,
        grid_spec=pltpu.PrefetchScalarGridSpec(
            num_scalar_prefetch=4,
            grid=(n_tiles,),
            in_specs=[
                pl.BlockSpec((r, D_MODEL), x_map),
                pl.BlockSpec(w_up, lo_map, pipeline_mode=one),
                pl.BlockSpec(w_up, lo_map, pipeline_mode=one),
                pl.BlockSpec(w_dn, lo_map, pipeline_mode=one),
                pl.BlockSpec(w_up, hi_map, pipeline_mode=one),
                pl.BlockSpec(w_up, hi_map, pipeline_mode=one),
                pl.BlockSpec(w_dn, hi_map, pipeline_mode=one),
            ],
            out_specs=pl.BlockSpec((r, D_MODEL), lambda i, lo, hi, blk, valid: (i, 0)),
        ),
        out_shape=jax.ShapeDtypeStruct((n_tiles * r, D_MODEL), jnp.uint32),
        compiler_params=_cparams(("arbitrary",)),
        name="experts",
    )(lo, hi, blk, valid, xs, wg, wu, wd, wg, wu, wd)


def _combine_kernel(pos_ref, x1_ref, info_ref, eo_ref, y_ref, gbuf, sem):
    tm = x1_ref.shape[0]

    def row_copy(r):
        return pltpu.make_async_copy(eo_ref.at[pl.ds(pos_ref[0, 0, r], 1)], gbuf.at[pl.ds(r, 1)], sem)

    def start(r, carry):
        row_copy(r).start()
        return carry

    lax.fori_loop(0, tm, start, 0)

    def wait(r, carry):
        row_copy(r).wait()
        return carry

    lax.fori_loop(0, tm, wait, 0)

    g = gbuf[...]
    e_lo = pltpu.unpack_elementwise(g, index=0, packed_dtype=BF16, unpacked_dtype=F32)
    e_hi = pltpu.unpack_elementwise(g, index=1, packed_dtype=BF16, unpacked_dtype=F32)
    info = info_ref[...]
    y_ref[...] = x1_ref[...] + info[:, 1:2] * e_lo + info[:, 2:3] * e_hi


def _combine(pos, x1, info, eo):
    t = x1.shape[0]
    tm = TOKEN_TILE
    return pl.pallas_call(
        _combine_kernel,
        grid=(t // tm,),
        in_specs=[
            pl.BlockSpec((1, 1, tm), lambda i: (i, 0, 0), memory_space=pltpu.SMEM),
            pl.BlockSpec((tm, D_MODEL), lambda i: (i, 0)),
            pl.BlockSpec((tm, LANES), lambda i: (i, 0)),
            pl.BlockSpec(memory_space=pl.ANY),
        ],
        out_specs=pl.BlockSpec((tm, D_MODEL), lambda i: (i, 0)),
        out_shape=jax.ShapeDtypeStruct((t, D_MODEL), F32),
        scratch_shapes=[pltpu.VMEM((tm, D_MODEL), jnp.uint32), pltpu.SemaphoreType.DMA],
        compiler_params=_cparams(("arbitrary",)),
        name="combine",
    )(pos, x1, info, eo)


def _rope_tables(n):
    rows = n // GRID_W
    row_id = np.repeat(np.arange(rows, dtype=np.float64), GRID_W)
    col_id = np.tile(np.arange(GRID_W, dtype=np.float64), rows)
    inv_freq = np.exp(-math.log(ROPE_THETA) * 2.0 * np.arange(ROPE_FREQS, dtype=np.float64) / (2 * ROPE_FREQS))
    ar = row_id[:, None] * inv_freq
    ac = col_id[:, None] * inv_freq
    cr, sr, ccol, scol = np.cos(ar), np.sin(ar), np.cos(ac), np.sin(ac)
    zero = np.zeros_like(sr)
    c = np.concatenate([cr, cr, ccol, ccol], axis=1)
    s1 = np.concatenate([-sr, zero, -scol, zero], axis=1)
    s2 = np.concatenate([zero, sr, zero, scol], axis=1)
    return jnp.asarray(c, F32), jnp.asarray(s1, F32), jnp.asarray(s2, F32)


def _channel_dft_tables():
    g = FOURIER_GROUP_DIM
    a = 2.0 * np.pi * ((np.arange(g)[:, None] * np.arange(g)[None, :]) % g) / g
    return jnp.asarray(np.cos(a), F32), jnp.asarray(np.sin(a), F32)


_PAIR_LO = np.array([0, 0, 0, 1, 1, 2], np.int32)
_PAIR_HI = np.array([1, 2, 3, 2, 3, 3], np.int32)


def kernel(x_prompt, x_sample, norm1_g, w_in, q_norm_g, k_norm_g, w_fmix, attn_out_g, fourier_out_g,
           w_out, norm2_g, w_router_group, b_router_group, w_router_expert, b_router_expert,
           w_gate, w_up, w_down):
    l = 0
    g1 = norm1_g[l][None, :]
    g2 = norm2_g[l][None, :]
    qg = q_norm_g[l][None, :]
    kg = k_norm_g[l][None, :]
    ga = attn_out_g[l][None, :]
    gf = fourier_out_g[l][None, :]
    w_in_b = w_in[l].astype(BF16)
    w_fmix_b = w_fmix[l].astype(BF16)
    w_out_b = w_out[l].astype(BF16)
    wg_b = w_gate[l].astype(BF16)
    wu_b = w_up[l].astype(BF16)
    wd_b = w_down[l].astype(BF16)
    n_r = N_EXPERT_GROUPS + N_EXPERTS
    w_router = jnp.zeros((D_MODEL, LANES), F32)
    w_router = w_router.at[:, :N_EXPERT_GROUPS].set(w_router_group[l])
    w_router = w_router.at[:, N_EXPERT_GROUPS:n_r].set(w_router_expert[l]).astype(BF16)
    b_router = jnp.zeros((1, LANES), F32)
    b_router = b_router.at[0, :N_EXPERT_GROUPS].set(b_router_group[l])
    b_router = b_router.at[0, N_EXPERT_GROUPS:n_r].set(b_router_expert[l])
    cc, sc = _channel_dft_tables()

    seqs = []
    for x in (x_prompt, x_sample):
        batch, n, _ = x.shape
        x2d = x.reshape(batch * n, D_MODEL)
        rope_c, rope_s1, rope_s2 = _rope_tables(n)
        q, k, vt, f = _in_proj(x2d, n, g1, w_in_b, qg, kg, rope_c, rope_s1, rope_s2)
        attn = _attention(q.reshape(batch, n, ATTN_WIDTH), k.reshape(batch, n, KV_WIDTH), vt)
        u, v = _position_dft(f.reshape(batch, n, FOURIER_WIDTH), n)
        x1, h2, info, cnt = _out_proj(x2d, attn.reshape(batch * n, ATTN_WIDTH), u, v, cc, sc, w_fmix_b,
                                      ga, gf, w_out_b, g2, w_router, b_router)
        seqs.append((x.shape, x1, h2, info, cnt))

    r = EXPERT_TILE
    t_all = sum(s[1].shape[0] for s in seqs)
    n_tiles = t_all // r + N_BUCKETS
    counts = sum(s[4][0, :N_BUCKETS] for s in seqs).astype(jnp.int32)
    tiles_b = (counts + (r - 1)) // r
    tile_end = jnp.cumsum(tiles_b)
    tile_start = tile_end - tiles_b
    total_tiles = tile_end[-1]
    starts = jnp.zeros((32,), F32).at[:N_BUCKETS].set((tile_start * r).astype(F32))
    starts = jnp.broadcast_to(starts[:, None], (32, LANES))
    tile_id = jnp.arange(n_tiles, dtype=jnp.int32)
    blk = jnp.minimum(tile_id, total_tiles - 1)
    tile_bucket = jnp.sum((tile_end[None, :] <= blk[:, None]).astype(jnp.int32), axis=1)
    grp = tile_bucket // N_PAIRS
    pair = tile_bucket % N_PAIRS
    lo = grp * EXPERTS_PER_GROUP + jnp.asarray(_PAIR_LO)[pair]
    hi = grp * EXPERTS_PER_GROUP + jnp.asarray(_PAIR_HI)[pair]
    valid = (tile_id < total_tiles).astype(jnp.int32)

    info_all = jnp.concatenate([s[3] for s in seqs], axis=0)
    pos = _rank(info_all, starts)

    xs = jnp.zeros((n_tiles * r, D_MODEL // LANES, LANES), BF16)
    off = 0
    for _, _, h2, _, _ in seqs:
        nt = h2.shape[0] // TOKEN_TILE
        xs = _scatter(pos[off:off + nt], h2, xs)
        off += nt
    eo = _experts(lo, hi, blk, valid, xs.reshape(n_tiles * r, D_MODEL), wg_b, wu_b, wd_b)

    outs = []
    off = 0
    for shape, x1, _, info, _ in seqs:
        nt = x1.shape[0] // TOKEN_TILE
        y = _combine(pos[off:off + nt], x1, info, eo)
        outs.append(y.reshape(shape))
        off += nt
    return tuple(outs)
```

```python
import functools
import math

import numpy as np
import jax
import jax.numpy as jnp
from jax import lax
from jax.experimental import pallas as pl
from jax.experimental.pallas import tpu as pltpu

F32 = jnp.float32
BF16 = jnp.bfloat16

EPS = 1e-6
D_MODEL = 2048
HEAD_DIM = 128
N_HEADS = 8
N_KV_HEADS = 2
Q_PER_KV = N_HEADS // N_KV_HEADS
ATTN_WIDTH = N_HEADS * HEAD_DIM
KV_WIDTH = N_KV_HEADS * HEAD_DIM
FOURIER_WIDTH = D_MODEL - ATTN_WIDTH
N_FOURIER_GROUPS = 4
FOURIER_GROUP_DIM = FOURIER_WIDTH // N_FOURIER_GROUPS
IN_WIDTH = ATTN_WIDTH + 2 * KV_WIDTH + FOURIER_WIDTH
GRID_W = 64
ROPE_THETA = 10000.0
ROPE_FREQS = HEAD_DIM // 4
N_EXPERT_GROUPS = 4
EXPERTS_PER_GROUP = 4
N_EXPERTS = N_EXPERT_GROUPS * EXPERTS_PER_GROUP
EXPERT_FF = D_MODEL // 2
N_PAIRS = 6
N_BUCKETS = N_EXPERT_GROUPS * N_PAIRS

LANES = 128
TOKEN_TILE = 512
ATTN_TILE = 1024
EXPERT_TILE = 256
DFT_N2 = 32
DFT3_ROWS = 16
ROW_DMA_UNROLL = 8
VMEM_LIMIT = 56 * 1024 * 1024
NEG_BIG = -1e30
LOG2E = 1.4426950408889634
SCORE_HEADROOM = 64
SCORE_BOUND_MAX = 87.0


def _cparams(sem, vmem=VMEM_LIMIT):
    return pltpu.CompilerParams(dimension_semantics=sem, vmem_limit_bytes=vmem)


def _const_spec(shape):
    nd = len(shape)
    return pl.BlockSpec(shape, lambda *_: (0,) * nd, pipeline_mode=pl.Buffered(1))


def _inproj_kernel(x_ref, g1_ref, w_ref, qg_ref, kg_ref, c_ref, s1_ref, s2_ref, perm_ref,
                   q_ref, k_ref, vt_ref, f_ref, *, q_scale):
    x = x_ref[...]
    ms = jnp.mean(x * x, axis=-1, keepdims=True)
    h = (x * lax.rsqrt(ms + EPS) * g1_ref[...]).astype(BF16)
    c = c_ref[...]
    s1 = s1_ref[...]
    s2 = s2_ref[...]

    def norm_rope(z, g):
        zms = jnp.mean(z * z, axis=-1, keepdims=True)
        zn = z * lax.rsqrt(zms + EPS) * g
        return zn * c + pltpu.roll(zn, 96, 1) * s1 + pltpu.roll(zn, 32, 1) * s2

    qg = qg_ref[...]
    kg = kg_ref[...]
    pq = jnp.dot(h, w_ref[:, 0:ATTN_WIDTH], preferred_element_type=F32)
    for hh in range(N_HEADS):
        sl = slice(hh * HEAD_DIM, (hh + 1) * HEAD_DIM)
        q_ref[:, sl] = (norm_rope(pq[:, sl], qg) * q_scale).astype(BF16)
    pk = jnp.dot(h, w_ref[:, ATTN_WIDTH:ATTN_WIDTH + KV_WIDTH], preferred_element_type=F32)
    for hh in range(N_KV_HEADS):
        sl = slice(hh * HEAD_DIM, (hh + 1) * HEAD_DIM)
        k_ref[:, sl] = norm_rope(pk[:, sl], kg).astype(BF16)
    pv = jnp.dot(h, w_ref[:, ATTN_WIDTH + KV_WIDTH:ATTN_WIDTH + 2 * KV_WIDTH],
                 preferred_element_type=F32)
    vt_ref[0] = pv.T.astype(BF16)
    pf = jnp.dot(h, w_ref[:, ATTN_WIDTH + 2 * KV_WIDTH:IN_WIDTH], preferred_element_type=F32)
    pfp = jnp.dot(perm_ref[...], pf.astype(BF16), preferred_element_type=F32).astype(BF16)
    rows = pfp.shape[0] // DFT_N2
    for t2 in range(DFT_N2):
        f_ref[0, :, t2 * FOURIER_WIDTH:(t2 + 1) * FOURIER_WIDTH] = pfp[t2 * rows:(t2 + 1) * rows, :]


def _in_proj(x2d, n, g1, w_in, qg, kg, rope_c, rope_s1, rope_s2):
    t = x2d.shape[0]
    tm = TOKEN_TILE
    nt = n // tm
    batch = t // n
    q_scale = LOG2E / math.sqrt(HEAD_DIM)
    r_out = np.arange(tm)
    perm = np.zeros((tm, tm), np.float32)
    perm[r_out, (r_out % (tm // DFT_N2)) * DFT_N2 + r_out // (tm // DFT_N2)] = 1.0
    perm = jnp.asarray(perm, BF16)
    return pl.pallas_call(
        functools.partial(_inproj_kernel, q_scale=q_scale),
        grid=(t // tm,),
        in_specs=[
            pl.BlockSpec((tm, D_MODEL), lambda i: (i, 0)),
            _const_spec((1, D_MODEL)),
            _const_spec((D_MODEL, IN_WIDTH)),
            _const_spec((1, HEAD_DIM)),
            _const_spec((1, HEAD_DIM)),
            pl.BlockSpec((tm, HEAD_DIM), lambda i: (i % nt, 0)),
            pl.BlockSpec((tm, HEAD_DIM), lambda i: (i % nt, 0)),
            pl.BlockSpec((tm, HEAD_DIM), lambda i: (i % nt, 0)),
            _const_spec((tm, tm)),
        ],
        out_specs=[
            pl.BlockSpec((tm, ATTN_WIDTH), lambda i: (i, 0)),
            pl.BlockSpec((tm, KV_WIDTH), lambda i: (i, 0)),
            pl.BlockSpec((1, KV_WIDTH, tm), lambda i: (i // nt, 0, i % nt)),
            pl.BlockSpec((1, tm // DFT_N2, DFT_N2 * FOURIER_WIDTH), lambda i: (i // nt, i % nt, 0)),
        ],
        out_shape=[
            jax.ShapeDtypeStruct((t, ATTN_WIDTH), BF16),
            jax.ShapeDtypeStruct((t, KV_WIDTH), BF16),
            jax.ShapeDtypeStruct((batch, KV_WIDTH, n), BF16),
            jax.ShapeDtypeStruct((batch, n // DFT_N2, DFT_N2 * FOURIER_WIDTH), BF16),
        ],
        compiler_params=_cparams(("arbitrary",)),
        name="in_proj",
    )(x2d, g1, w_in, qg, kg, rope_c, rope_s1, rope_s2, perm)


def _attn_kernel(sb_ref, q_ref, k_ref, vt_ref, o_ref, m_ref, l_ref, acc_ref, *, nk):
    ki = pl.program_id(3)
    bounded = sb_ref[0] == 1

    @pl.when(ki == 0)
    def _():
        m_ref[...] = jnp.full(m_ref.shape, NEG_BIG, F32)
        l_ref[...] = jnp.zeros(l_ref.shape, F32)
        acc_ref[...] = jnp.zeros(acc_ref.shape, F32)

    def scores(g):
        q = q_ref[0, :, g * HEAD_DIM:(g + 1) * HEAD_DIM]
        return lax.dot_general(k_ref[0], q, (((1,), (1,)), ((), ())), preferred_element_type=F32)

    @pl.when(bounded)
    def _():
        shift = sb_ref[1].astype(F32)
        for g in range(Q_PER_KV):
            p = jnp.exp2(scores(g) - shift)
            l_ref[g] = l_ref[g] + jnp.sum(p, axis=0, keepdims=True)
            acc_ref[g] = acc_ref[g] + jnp.dot(vt_ref[0], p.astype(BF16), preferred_element_type=F32)

    @pl.when(jnp.logical_not(bounded))
    def _():
        for g in range(Q_PER_KV):
            s = scores(g)
            m_prev = m_ref[g]
            m_new = jnp.maximum(m_prev, jnp.max(s, axis=0, keepdims=True))
            alpha = jnp.exp2(m_prev - m_new)
            p = jnp.exp2(s - m_new)
            l_ref[g] = alpha * l_ref[g] + jnp.sum(p, axis=0, keepdims=True)
            acc_ref[g] = alpha * acc_ref[g] + jnp.dot(vt_ref[0], p.astype(BF16),
                                                      preferred_element_type=F32)
            m_ref[g] = m_new

    @pl.when(ki == nk - 1)
    def _():
        for g in range(Q_PER_KV):
            o = acc_ref[g] * (1.0 / l_ref[g])
            o_ref[0, :, g * HEAD_DIM:(g + 1) * HEAD_DIM] = o.T.astype(BF16)


def _score_bound(qg, kg):
    ub = (HEAD_DIM * jnp.max(jnp.abs(qg)) * jnp.max(jnp.abs(kg))
          * (LOG2E / math.sqrt(HEAD_DIM)) * (1.0 + 1e-3))
    ok = ub <= SCORE_BOUND_MAX
    shift = jnp.where(ok, jnp.ceil(ub) - SCORE_HEADROOM, 0.0)
    return jnp.stack([ok.astype(jnp.int32), shift.astype(jnp.int32)])


def _attention(q, k, vt, score_bound):
    batch, n, _ = q.shape
    tq = tk = min(ATTN_TILE, n)
    gw = Q_PER_KV * HEAD_DIM
    return pl.pallas_call(
        functools.partial(_attn_kernel, nk=n // tk),
        grid_spec=pltpu.PrefetchScalarGridSpec(
            num_scalar_prefetch=1,
            grid=(batch, N_KV_HEADS, n // tq, n // tk),
            in_specs=[
                pl.BlockSpec((1, tq, gw), lambda b, h, qi, ki, sb: (b, qi, h)),
                pl.BlockSpec((1, tk, HEAD_DIM), lambda b, h, qi, ki, sb: (b, ki, h)),
                pl.BlockSpec((1, HEAD_DIM, tk), lambda b, h, qi, ki, sb: (b, h, ki)),
            ],
            out_specs=pl.BlockSpec((1, tq, gw), lambda b, h, qi, ki, sb: (b, qi, h)),
            scratch_shapes=[
                pltpu.VMEM((Q_PER_KV, 1, tq), F32),
                pltpu.VMEM((Q_PER_KV, 1, tq), F32),
                pltpu.VMEM((Q_PER_KV, HEAD_DIM, tq), F32),
            ],
        ),
        out_shape=jax.ShapeDtypeStruct((batch, n, ATTN_WIDTH), BF16),
        compiler_params=_cparams(("arbitrary", "arbitrary", "arbitrary", "arbitrary")),
        name="attention",
    )(score_bound, q, k, vt)


def _dft1_kernel(f_ref, fc_ref, twc_ref, tws_ref, b_ref, *, n1):
    x = f_ref[0]
    a = jnp.dot(fc_ref[...].astype(BF16), x, preferred_element_type=F32)
    ar = a[:n1]
    ai = a[n1:]
    c = twc_ref[0]
    s = tws_ref[0]
    for j in range(x.shape[1] // LANES):
        sl = slice(j * LANES, (j + 1) * LANES)
        br = ar[:, sl] * c + ai[:, sl] * s
        bi = ai[:, sl] * c - ar[:, sl] * s
        b_ref[0, 0, :, sl] = pltpu.pack_elementwise([br, bi], packed_dtype=BF16)


def _dft3_kernel(b_ref, f3_ref, u_ref, v_ref, rows_ref, us_ref, vs_ref):
    n2, kc, c = b_ref.shape[1:]
    n_lane_tiles = c // LANES
    for jc in range(n_lane_tiles):
        rows_ref[jc] = b_ref[0, :, :, jc * LANES:(jc + 1) * LANES].reshape(n2 * kc, LANES)
    f3 = f3_ref[...].astype(BF16)
    for j in range(kc):
        w = jnp.concatenate([rows_ref[jc, pl.ds(j, n2, stride=kc), :] for jc in range(n_lane_tiles)],
                            axis=1)
        br = pltpu.unpack_elementwise(w, index=0, packed_dtype=BF16, unpacked_dtype=F32)
        bi = pltpu.unpack_elementwise(w, index=1, packed_dtype=BF16, unpacked_dtype=F32)
        rhs = jnp.concatenate([br, bi], axis=0).astype(BF16)
        uv = jnp.dot(f3, rhs, preferred_element_type=F32)
        for jc in range(n_lane_tiles):
            sl = slice(jc * LANES, (jc + 1) * LANES)
            us_ref[jc, pl.ds(j, n2, stride=kc), :] = uv[:n2, sl]
            vs_ref[jc, pl.ds(j, n2, stride=kc), :] = uv[n2:, sl]
    for jc in range(n_lane_tiles):
        sl = slice(jc * LANES, (jc + 1) * LANES)
        u_ref[0, :, :, sl] = us_ref[jc].reshape(n2, kc, LANES).astype(BF16)
        v_ref[0, :, :, sl] = vs_ref[jc].reshape(n2, kc, LANES).astype(BF16)


def _dft_tables(n):
    n2 = DFT_N2
    n1 = n // n2
    scale = 1.0 / math.sqrt(n * FOURIER_GROUP_DIM)
    k1 = np.arange(n1)[:, None]
    a1 = 2.0 * np.pi * ((k1 * np.arange(n1)[None, :]) % n1) / n1
    fcat = np.concatenate([np.cos(a1), -np.sin(a1)], axis=0) * scale
    th = 2.0 * np.pi * ((np.arange(n2)[:, None] * k1.T) % n) / n
    twc = np.repeat(np.cos(th)[:, :, None], LANES, axis=2)
    tws = np.repeat(np.sin(th)[:, :, None], LANES, axis=2)
    a3 = 2.0 * np.pi * ((np.arange(n2)[:, None] * np.arange(n2)[None, :]) % n2) / n2
    c3, s3 = np.cos(a3), np.sin(a3)
    f3 = np.block([[c3, s3], [-s3, c3]])
    return jnp.asarray(fcat, F32), jnp.asarray(twc, F32), jnp.asarray(tws, F32), jnp.asarray(f3, F32)


def _position_dft(f2, n):
    batch = f2.shape[0]
    c = FOURIER_WIDTH
    n2 = DFT_N2
    n1 = n // n2
    kc = DFT3_ROWS
    fcat, twc, tws, f3 = _dft_tables(n)
    b = pl.pallas_call(
        functools.partial(_dft1_kernel, n1=n1),
        grid=(batch, n2),
        in_specs=[
            pl.BlockSpec((1, n1, c), lambda b, j: (b, 0, j)),
            _const_spec((2 * n1, n1)),
            pl.BlockSpec((1, n1, LANES), lambda b, j: (j, 0, 0)),
            pl.BlockSpec((1, n1, LANES), lambda b, j: (j, 0, 0)),
        ],
        out_specs=pl.BlockSpec((1, 1, n1, c), lambda b, j: (b, j, 0, 0)),
        out_shape=jax.ShapeDtypeStruct((batch, n2, n1, c), jnp.uint32),
        compiler_params=_cparams(("arbitrary", "arbitrary")),
        name="dft_stage1",
    )(f2, fcat, twc, tws)
    blk = pl.BlockSpec((1, n2, kc, c), lambda b, j: (b, 0, j, 0))
    u, v = pl.pallas_call(
        _dft3_kernel,
        grid=(batch, n1 // kc),
        in_specs=[blk, _const_spec((2 * n2, 2 * n2))],
        out_specs=[blk, blk],
        out_shape=[jax.ShapeDtypeStruct((batch, n2, n1, c), BF16)] * 2,
        scratch_shapes=[
            pltpu.VMEM((c // LANES, n2 * kc, LANES), jnp.uint32),
            pltpu.VMEM((c // LANES, n2 * kc, LANES), F32),
            pltpu.VMEM((c // LANES, n2 * kc, LANES), F32),
        ],
        compiler_params=_cparams(("arbitrary", "arbitrary")),
        name="dft_stage3",
    )(b, f3)
    return u.reshape(batch * n, c), v.reshape(batch * n, c)


def _rms(z, g):
    ms = jnp.mean(z * z, axis=-1, keepdims=True)
    return z * lax.rsqrt(ms + EPS) * g


def _outproj_kernel(x_ref, a_ref, u_ref, v_ref, cc_ref, sc_ref, wf_ref, ga_ref, gf_ref, wo_ref,
                    g2_ref, wr_ref, br_ref,
                    x1_ref, h2_ref, info_ref, cnt_ref):
    i = pl.program_id(0)
    cc = cc_ref[...].astype(BF16)
    sc = sc_ref[...].astype(BF16)
    parts = []
    for g in range(N_FOURIER_GROUPS):
        sl = slice(g * FOURIER_GROUP_DIM, (g + 1) * FOURIER_GROUP_DIM)
        parts.append(jnp.dot(u_ref[:, sl], cc, preferred_element_type=F32)
                     + jnp.dot(v_ref[:, sl], sc, preferred_element_type=F32))
    fmix = jnp.concatenate(parts, axis=-1).astype(BF16)
    four = jnp.dot(fmix, wf_ref[...], preferred_element_type=F32)
    fn = _rms(four, gf_ref[...]).astype(BF16)
    an = _rms(a_ref[...].astype(F32), ga_ref[...]).astype(BF16)
    mix = (jnp.dot(an, wo_ref[0:ATTN_WIDTH, :], preferred_element_type=F32)
           + jnp.dot(fn, wo_ref[ATTN_WIDTH:D_MODEL, :], preferred_element_type=F32))
    x1 = x_ref[...] + mix
    x1_ref[...] = x1
    h2f = _rms(x1, g2_ref[...])
    h2 = h2f.astype(BF16)
    h2_ref[...] = pltpu.pack_elementwise([h2f[:, :D_MODEL // 2], h2f[:, D_MODEL // 2:]], packed_dtype=BF16)

    lg = jnp.dot(h2, wr_ref[...], preferred_element_type=F32) + br_ref[...]
    lane = lax.broadcasted_iota(jnp.int32, lg.shape, 1).astype(F32)
    is_g = lane < N_EXPERT_GROUPS
    gl = jnp.where(is_g, lg, NEG_BIG)
    gmax = jnp.max(gl, axis=-1, keepdims=True)
    gidx = jnp.min(jnp.where(gl == gmax, lane, float(LANES)), axis=-1, keepdims=True)
    gsum = jnp.sum(jnp.where(is_g, jnp.exp(gl - gmax), 0.0), axis=-1, keepdims=True)
    g_w = 1.0 / gsum
    lo_lane = N_EXPERT_GROUPS + EXPERTS_PER_GROUP * gidx
    in_grp = jnp.logical_and(lane >= lo_lane, lane < lo_lane + EXPERTS_PER_GROUP)
    el = jnp.where(in_grp, lg, NEG_BIG)
    m1 = jnp.max(el, axis=-1, keepdims=True)
    i1 = jnp.min(jnp.where(el == m1, lane, float(LANES)), axis=-1, keepdims=True)
    el2 = jnp.where(lane == i1, NEG_BIG, el)
    m2 = jnp.max(el2, axis=-1, keepdims=True)
    i2 = jnp.min(jnp.where(el2 == m2, lane, float(LANES)), axis=-1, keepdims=True)
    r = jnp.exp(m2 - m1)
    w1 = g_w / (1.0 + r)
    w2 = g_w * r / (1.0 + r)
    first_lo = i1 < i2
    ea = jnp.minimum(i1, i2) - lo_lane
    eb = jnp.maximum(i1, i2) - lo_lane
    pair = ea * (7.0 - ea) * 0.5 + (eb - ea - 1.0)
    bucket = gidx * float(N_PAIRS) + pair
    w_lo = jnp.where(first_lo, w1, w2)
    w_hi = jnp.where(first_lo, w2, w1)
    info_ref[...] = jnp.where(lane == 0.0, bucket,
                              jnp.where(lane == 1.0, w_lo, jnp.where(lane == 2.0, w_hi, 0.0)))
    cnt = jnp.sum(jnp.where(lane == bucket, 1.0, 0.0), axis=0, keepdims=True)

    @pl.when(i == 0)
    def _():
        cnt_ref[...] = jnp.zeros(cnt_ref.shape, F32)

    cnt_ref[...] = cnt_ref[...] + cnt


def _out_proj(x2d, attn, u, v, cc, sc, w_fmix, ga, gf, w_out, g2, w_router, b_router):
    t = x2d.shape[0]
    tm = TOKEN_TILE
    row = lambda i: (i, 0)
    return pl.pallas_call(
        _outproj_kernel,
        grid=(t // tm,),
        in_specs=[
            pl.BlockSpec((tm, D_MODEL), row),
            pl.BlockSpec((tm, ATTN_WIDTH), row),
            pl.BlockSpec((tm, FOURIER_WIDTH), row),
            pl.BlockSpec((tm, FOURIER_WIDTH), row),
            _const_spec((FOURIER_GROUP_DIM, FOURIER_GROUP_DIM)),
            _const_spec((FOURIER_GROUP_DIM, FOURIER_GROUP_DIM)),
            _const_spec((FOURIER_WIDTH, FOURIER_WIDTH)),
            _const_spec((1, ATTN_WIDTH)),
            _const_spec((1, FOURIER_WIDTH)),
            _const_spec((D_MODEL, D_MODEL)),
            _const_spec((1, D_MODEL)),
            _const_spec((D_MODEL, LANES)),
            _const_spec((1, LANES)),
        ],
        out_specs=[
            pl.BlockSpec((tm, D_MODEL), row),
            pl.BlockSpec((tm, D_MODEL // 2), row),
            pl.BlockSpec((tm, LANES), row),
            pl.BlockSpec((8, LANES), lambda i: (0, 0)),
        ],
        out_shape=[
            jax.ShapeDtypeStruct((t, D_MODEL), F32),
            jax.ShapeDtypeStruct((t, D_MODEL // 2), jnp.uint32),
            jax.ShapeDtypeStruct((t, LANES), F32),
            jax.ShapeDtypeStruct((8, LANES), F32),
        ],
        compiler_params=_cparams(("arbitrary",)),
        name="out_proj_router",
    )(x2d, attn, u, v, cc, sc, w_fmix, ga, gf, w_out, g2, w_router, b_router)


def _rank_kernel(info_ref, start_ref, pos_ref, base_ref):
    i = pl.program_id(0)
    tm = info_ref.shape[0]

    @pl.when(i == 0)
    def _():
        base_ref[...] = start_ref[...]

    bucket = info_ref[...].T[0:1, :]
    brow = lax.broadcasted_iota(jnp.int32, (32, tm), 0).astype(F32)
    onehot = jnp.where(brow == bucket, 1.0, 0.0)
    oh = onehot.astype(BF16)
    src = lax.broadcasted_iota(jnp.int32, (tm, tm), 0)
    dst = lax.broadcasted_iota(jnp.int32, (tm, tm), 1)
    upper = jnp.where(src <= dst, 1.0, 0.0).astype(BF16)
    cum = jnp.dot(oh, upper, preferred_element_type=F32)
    base = base_ref[...]
    base_t = jnp.concatenate([base] * (tm // LANES), axis=1)
    pos = jnp.sum(onehot * (cum - 1.0 + base_t), axis=0, keepdims=True)
    pos_ref[0] = pos.astype(jnp.int32)
    ones = jnp.ones((tm, LANES), BF16)
    base_ref[...] = base + jnp.dot(oh, ones, preferred_element_type=F32)


def _rank(info, starts):
    t = info.shape[0]
    tm = TOKEN_TILE
    return pl.pallas_call(
        _rank_kernel,
        grid=(t // tm,),
        in_specs=[
            pl.BlockSpec((tm, LANES), lambda i: (i, 0)),
            pl.BlockSpec((32, LANES), lambda i: (0, 0)),
        ],
        out_specs=pl.BlockSpec((1, 1, tm), lambda i: (i, 0, 0)),
        out_shape=jax.ShapeDtypeStruct((t // tm, 1, tm), jnp.int32),
        scratch_shapes=[pltpu.VMEM((32, LANES), F32)],
        compiler_params=_cparams(("arbitrary",)),
        name="rank",
    )(info, starts)


def _for_each_row(n_rows, fn):
    def body(c, carry):
        for u in range(ROW_DMA_UNROLL):
            fn(c * ROW_DMA_UNROLL + u)
        return carry

    lax.fori_loop(0, n_rows // ROW_DMA_UNROLL, body, 0)


def _scatter_kernel(pos_ref, h_ref, xs_in_ref, xs_ref, sem):
    del xs_in_ref
    tm = h_ref.shape[0]

    def row_copy(r):
        return pltpu.make_async_copy(h_ref.at[pl.ds(r, 1)], xs_ref.at[pl.ds(pos_ref[0, 0, r], 1)], sem)

    _for_each_row(tm, lambda r: row_copy(r).start())
    _for_each_row(tm, lambda r: row_copy(r).wait())


def _scatter(pos, h2, xs):
    t, w = h2.shape
    tm = TOKEN_TILE
    return pl.pallas_call(
        _scatter_kernel,
        grid=(t // tm,),
        in_specs=[
            pl.BlockSpec((1, 1, tm), lambda i: (i, 0, 0), memory_space=pltpu.SMEM),
            pl.BlockSpec((tm, w), lambda i: (i, 0)),
            pl.BlockSpec(memory_space=pl.ANY),
        ],
        out_specs=pl.BlockSpec(memory_space=pl.ANY),
        out_shape=jax.ShapeDtypeStruct(xs.shape, xs.dtype),
        scratch_shapes=[pltpu.SemaphoreType.DMA],
        input_output_aliases={2: 0},
        compiler_params=_cparams(("arbitrary",)),
        name="scatter_rows",
    )(pos, h2, xs)


def _expert_kernel(lo_ref, hi_ref, blk_ref, valid_ref, x_ref,
                   wg0_ref, wu0_ref, wd0_ref, wg1_ref, wu1_ref, wd1_ref, o_ref):
    del lo_ref, hi_ref, blk_ref
    i = pl.program_id(0)

    @pl.when(valid_ref[i] == 1)
    def _():
        xw = x_ref[...]
        x = jnp.concatenate(
            [pltpu.unpack_elementwise(xw, index=0, packed_dtype=BF16, unpacked_dtype=F32),
             pltpu.unpack_elementwise(xw, index=1, packed_dtype=BF16, unpacked_dtype=F32)],
            axis=1).astype(BF16)

        def mlp(wg_ref, wu_ref, wd_ref):
            a = jnp.dot(x, wg_ref[0], preferred_element_type=F32)
            u = jnp.dot(x, wu_ref[0], preferred_element_type=F32)
            h = (a * (1.0 / (1.0 + jnp.exp(-a))) * u).astype(BF16)
            return jnp.dot(h, wd_ref[0], preferred_element_type=F32)

        e_lo = mlp(wg0_ref, wu0_ref, wd0_ref)
        e_hi = mlp(wg1_ref, wu1_ref, wd1_ref)
        o_ref[...] = pltpu.pack_elementwise([e_lo, e_hi], packed_dtype=BF16)

    @pl.when(valid_ref[i] == 0)
    def _():
        zero = jnp.zeros(o_ref.shape, F32)
        o_ref[...] = pltpu.pack_elementwise([zero, zero], packed_dtype=BF16)


def _experts(lo, hi, blk, valid, xs, wg, wu, wd):
    n_tiles = lo.shape[0]
    r = EXPERT_TILE
    x_map = lambda i, lo, hi, blk, valid: (blk[i], 0)
    lo_map = lambda i, lo, hi, blk, valid: (lo[i], 0, 0)
    hi_map = lambda i, lo, hi, blk, valid: (hi[i], 0, 0)
    one = pl.Buffered(1)
    w_up = (1, D_MODEL, EXPERT_FF)
    w_dn = (1, EXPERT_FF, D_MODEL)
    return pl.pallas_call(
        _expert_kernel,
        grid_spec=pltpu.PrefetchScalarGridSpec(
            num_scalar_prefetch=4,
            grid=(n_tiles,),
            in_specs=[
                pl.BlockSpec((r, D_MODEL // 2), x_map),
                pl.BlockSpec(w_up, lo_map, pipeline_mode=one),
                pl.BlockSpec(w_up, lo_map, pipeline_mode=one),
                pl.BlockSpec(w_dn, lo_map, pipeline_mode=one),
                pl.BlockSpec(w_up, hi_map, pipeline_mode=one),
                pl.BlockSpec(w_up, hi_map, pipeline_mode=one),
                pl.BlockSpec(w_dn, hi_map, pipeline_mode=one),
            ],
            out_specs=pl.BlockSpec((r, D_MODEL), lambda i, lo, hi, blk, valid: (i, 0)),
        ),
        out_shape=jax.ShapeDtypeStruct((n_tiles * r, D_MODEL), jnp.uint32),
        compiler_params=_cparams(("arbitrary",)),
        name="experts",
    )(lo, hi, blk, valid, xs, wg, wu, wd, wg, wu, wd)


def _combine_kernel(pos_ref, x1_ref, info_ref, eo_ref, y_ref, gbuf, sem):
    tm = x1_ref.shape[0]

    def row_copy(r):
        return pltpu.make_async_copy(eo_ref.at[pl.ds(pos_ref[0, 0, r], 1)], gbuf.at[pl.ds(r, 1)], sem)

    _for_each_row(tm, lambda r: row_copy(r).start())
    _for_each_row(tm, lambda r: row_copy(r).wait())

    g = gbuf[...]
    e_lo = pltpu.unpack_elementwise(g, index=0, packed_dtype=BF16, unpacked_dtype=F32)
    e_hi = pltpu.unpack_elementwise(g, index=1, packed_dtype=BF16, unpacked_dtype=F32)
    info = info_ref[...]
    y_ref[...] = x1_ref[...] + info[:, 1:2] * e_lo + info[:, 2:3] * e_hi


def _combine(pos, x1, info, eo):
    t = x1.shape[0]
    tm = TOKEN_TILE
    return pl.pallas_call(
        _combine_kernel,
        grid=(t // tm,),
        in_specs=[
            pl.BlockSpec((1, 1, tm), lambda i: (i, 0, 0), memory_space=pltpu.SMEM),
            pl.BlockSpec((tm, D_MODEL), lambda i: (i, 0)),
            pl.BlockSpec((tm, LANES), lambda i: (i, 0)),
            pl.BlockSpec(memory_space=pl.ANY),
        ],
        out_specs=pl.BlockSpec((tm, D_MODEL), lambda i: (i, 0)),
        out_shape=jax.ShapeDtypeStruct((t, D_MODEL), F32),
        scratch_shapes=[pltpu.VMEM((tm, D_MODEL), jnp.uint32), pltpu.SemaphoreType.DMA],
        compiler_params=_cparams(("arbitrary",)),
        name="combine",
    )(pos, x1, info, eo)


def _rope_tables(n):
    rows = n // GRID_W
    row_id = np.repeat(np.arange(rows, dtype=np.float64), GRID_W)
    col_id = np.tile(np.arange(GRID_W, dtype=np.float64), rows)
    inv_freq = np.exp(-math.log(ROPE_THETA) * 2.0 * np.arange(ROPE_FREQS, dtype=np.float64) / (2 * ROPE_FREQS))
    ar = row_id[:, None] * inv_freq
    ac = col_id[:, None] * inv_freq
    cr, sr, ccol, scol = np.cos(ar), np.sin(ar), np.cos(ac), np.sin(ac)
    zero = np.zeros_like(sr)
    c = np.concatenate([cr, cr, ccol, ccol], axis=1)
    s1 = np.concatenate([-sr, zero, -scol, zero], axis=1)
    s2 = np.concatenate([zero, sr, zero, scol], axis=1)
    return jnp.asarray(c, F32), jnp.asarray(s1, F32), jnp.asarray(s2, F32)


def _channel_dft_tables():
    g = FOURIER_GROUP_DIM
    a = 2.0 * np.pi * ((np.arange(g)[:, None] * np.arange(g)[None, :]) % g) / g
    return jnp.asarray(np.cos(a), F32), jnp.asarray(np.sin(a), F32)


_PAIR_LO = np.array([0, 0, 0, 1, 1, 2], np.int32)
_PAIR_HI = np.array([1, 2, 3, 2, 3, 3], np.int32)


def kernel(x_prompt, x_sample, norm1_g, w_in, q_norm_g, k_norm_g, w_fmix, attn_out_g, fourier_out_g,
           w_out, norm2_g, w_router_group, b_router_group, w_router_expert, b_router_expert,
           w_gate, w_up, w_down):
    l = 0
    g1 = norm1_g[l][None, :]
    g2 = norm2_g[l][None, :]
    qg = q_norm_g[l][None, :]
    kg = k_norm_g[l][None, :]
    ga = attn_out_g[l][None, :]
    gf = fourier_out_g[l][None, :]
    w_in_b = w_in[l].astype(BF16)
    w_fmix_b = w_fmix[l].astype(BF16)
    w_out_b = w_out[l].astype(BF16)
    wg_b = w_gate[l].astype(BF16)
    wu_b = w_up[l].astype(BF16)
    wd_b = w_down[l].astype(BF16)
    n_r = N_EXPERT_GROUPS + N_EXPERTS
    w_router = jnp.zeros((D_MODEL, LANES), F32)
    w_router = w_router.at[:, :N_EXPERT_GROUPS].set(w_router_group[l])
    w_router = w_router.at[:, N_EXPERT_GROUPS:n_r].set(w_router_expert[l]).astype(BF16)
    b_router = jnp.zeros((1, LANES), F32)
    b_router = b_router.at[0, :N_EXPERT_GROUPS].set(b_router_group[l])
    b_router = b_router.at[0, N_EXPERT_GROUPS:n_r].set(b_router_expert[l])
    cc, sc = _channel_dft_tables()
    score_bound = _score_bound(qg, kg)

    seqs = []
    for x in (x_prompt, x_sample):
        batch, n, _ = x.shape
        x2d = x.reshape(batch * n, D_MODEL)
        rope_c, rope_s1, rope_s2 = _rope_tables(n)
        q, k, vt, f2 = _in_proj(x2d, n, g1, w_in_b, qg, kg, rope_c, rope_s1, rope_s2)
        attn = _attention(q.reshape(batch, n, ATTN_WIDTH), k.reshape(batch, n, KV_WIDTH), vt, score_bound)
        u, v = _position_dft(f2, n)
        x1, h2, info, cnt = _out_proj(x2d, attn.reshape(batch * n, ATTN_WIDTH), u, v, cc, sc, w_fmix_b,
                                      ga, gf, w_out_b, g2, w_router, b_router)
        seqs.append((x.shape, x1, h2, info, cnt))

    r = EXPERT_TILE
    t_all = sum(s[1].shape[0] for s in seqs)
    n_tiles = t_all // r + N_BUCKETS
    counts = sum(s[4][0, :N_BUCKETS] for s in seqs).astype(jnp.int32)
    tiles_b = (counts + (r - 1)) // r
    tile_end = jnp.cumsum(tiles_b)
    tile_start = tile_end - tiles_b
    total_tiles = tile_end[-1]
    starts = jnp.zeros((32,), F32).at[:N_BUCKETS].set((tile_start * r).astype(F32))
    starts = jnp.broadcast_to(starts[:, None], (32, LANES))
    tile_id = jnp.arange(n_tiles, dtype=jnp.int32)
    blk = jnp.minimum(tile_id, total_tiles - 1)
    tile_bucket = jnp.sum((tile_end[None, :] <= blk[:, None]).astype(jnp.int32), axis=1)
    grp = tile_bucket // N_PAIRS
    pair = tile_bucket % N_PAIRS
    lo = grp * EXPERTS_PER_GROUP + jnp.asarray(_PAIR_LO)[pair]
    hi = grp * EXPERTS_PER_GROUP + jnp.asarray(_PAIR_HI)[pair]
    valid = (tile_id < total_tiles).astype(jnp.int32)

    info_all = jnp.concatenate([s[3] for s in seqs], axis=0)
    pos = _rank(info_all, starts)

    xs = jnp.zeros((n_tiles * r, D_MODEL // 2), jnp.uint32)
    off = 0
    for _, _, h2, _, _ in seqs:
        nt = h2.shape[0] // TOKEN_TILE
        xs = _scatter(pos[off:off + nt], h2, xs)
        off += nt
    eo = _experts(lo, hi, blk, valid, xs, wg_b, wu_b, wd_b)

    outs = []
    off = 0
    for shape, x1, _, info, _ in seqs:
        nt = x1.shape[0] // TOKEN_TILE
        y = _combine(pos[off:off + nt], x1, info, eo)
        outs.append(y.reshape(shape))
        off += nt
    return tuple(outs)
```

```python
import functools
import math

import numpy as np
import jax
import jax.numpy as jnp
from jax import lax
from jax.experimental import pallas as pl
from jax.experimental.pallas import tpu as pltpu

F32 = jnp.float32
BF16 = jnp.bfloat16

EPS = 1e-6
D_MODEL = 2048
HEAD_DIM = 128
N_HEADS = 8
N_KV_HEADS = 2
Q_PER_KV = N_HEADS // N_KV_HEADS
ATTN_WIDTH = N_HEADS * HEAD_DIM
KV_WIDTH = N_KV_HEADS * HEAD_DIM
FOURIER_WIDTH = D_MODEL - ATTN_WIDTH
N_FOURIER_GROUPS = 4
FOURIER_GROUP_DIM = FOURIER_WIDTH // N_FOURIER_GROUPS
IN_WIDTH = ATTN_WIDTH + 2 * KV_WIDTH + FOURIER_WIDTH
GRID_W = 64
ROPE_THETA = 10000.0
ROPE_FREQS = HEAD_DIM // 4
N_EXPERT_GROUPS = 4
EXPERTS_PER_GROUP = 4
N_EXPERTS = N_EXPERT_GROUPS * EXPERTS_PER_GROUP
EXPERT_FF = D_MODEL // 2
N_PAIRS = 6
N_BUCKETS = N_EXPERT_GROUPS * N_PAIRS

LANES = 128
TOKEN_TILE = 512
ATTN_TILE = 1024
EXPERT_TILE = 256
DFT_N2 = 32
DFT1_ROWS = 512
DFT3_ROWS = 16
ROW_DMA_UNROLL = 8
VMEM_LIMIT = 56 * 1024 * 1024
NEG_BIG = -1e30
LOG2E = 1.4426950408889634
SCORE_HEADROOM = 64
SCORE_BOUND_MAX = 87.0


def _cparams(sem, vmem=VMEM_LIMIT):
    return pltpu.CompilerParams(dimension_semantics=sem, vmem_limit_bytes=vmem)


def _const_spec(shape):
    nd = len(shape)
    return pl.BlockSpec(shape, lambda *_: (0,) * nd, pipeline_mode=pl.Buffered(1))


def _inproj_kernel(x_ref, g1_ref, w_ref, qg_ref, kg_ref, c_ref, s1_ref, s2_ref, perm_ref,
                   q_ref, k_ref, vt_ref, f_ref, *, q_scale):
    x = x_ref[...]
    ms = jnp.mean(x * x, axis=-1, keepdims=True)
    h = (x * lax.rsqrt(ms + EPS) * g1_ref[...]).astype(BF16)
    c = c_ref[...]
    s1 = s1_ref[...]
    s2 = s2_ref[...]

    def norm_rope(z, g):
        zms = jnp.mean(z * z, axis=-1, keepdims=True)
        zn = z * lax.rsqrt(zms + EPS) * g
        return zn * c + pltpu.roll(zn, 96, 1) * s1 + pltpu.roll(zn, 32, 1) * s2

    qg = qg_ref[...]
    kg = kg_ref[...]
    pq = jnp.dot(h, w_ref[:, 0:ATTN_WIDTH], preferred_element_type=F32)
    for hh in range(N_HEADS):
        sl = slice(hh * HEAD_DIM, (hh + 1) * HEAD_DIM)
        q_ref[:, sl] = (norm_rope(pq[:, sl], qg) * q_scale).astype(BF16)
    pk = jnp.dot(h, w_ref[:, ATTN_WIDTH:ATTN_WIDTH + KV_WIDTH], preferred_element_type=F32)
    for hh in range(N_KV_HEADS):
        sl = slice(hh * HEAD_DIM, (hh + 1) * HEAD_DIM)
        k_ref[:, sl] = norm_rope(pk[:, sl], kg).astype(BF16)
    pv = jnp.dot(h, w_ref[:, ATTN_WIDTH + KV_WIDTH:ATTN_WIDTH + 2 * KV_WIDTH],
                 preferred_element_type=F32)
    vt_ref[0] = pv.T.astype(BF16)
    pf = jnp.dot(h, w_ref[:, ATTN_WIDTH + 2 * KV_WIDTH:IN_WIDTH], preferred_element_type=F32)
    pfp = jnp.dot(perm_ref[...], pf.astype(BF16), preferred_element_type=F32).astype(BF16)
    rows = pfp.shape[0] // DFT_N2
    for t2 in range(DFT_N2):
        f_ref[0, :, t2 * FOURIER_WIDTH:(t2 + 1) * FOURIER_WIDTH] = pfp[t2 * rows:(t2 + 1) * rows, :]


def _in_proj(x2d, n, g1, w_in, qg, kg, rope_c, rope_s1, rope_s2):
    t = x2d.shape[0]
    tm = TOKEN_TILE
    nt = n // tm
    batch = t // n
    q_scale = LOG2E / math.sqrt(HEAD_DIM)
    r_out = np.arange(tm)
    perm = np.zeros((tm, tm), np.float32)
    perm[r_out, (r_out % (tm // DFT_N2)) * DFT_N2 + r_out // (tm // DFT_N2)] = 1.0
    perm = jnp.asarray(perm, BF16)
    return pl.pallas_call(
        functools.partial(_inproj_kernel, q_scale=q_scale),
        grid=(t // tm,),
        in_specs=[
            pl.BlockSpec((tm, D_MODEL), lambda i: (i, 0)),
            _const_spec((1, D_MODEL)),
            _const_spec((D_MODEL, IN_WIDTH)),
            _const_spec((1, HEAD_DIM)),
            _const_spec((1, HEAD_DIM)),
            pl.BlockSpec((tm, HEAD_DIM), lambda i: (i % nt, 0)),
            pl.BlockSpec((tm, HEAD_DIM), lambda i: (i % nt, 0)),
            pl.BlockSpec((tm, HEAD_DIM), lambda i: (i % nt, 0)),
            _const_spec((tm, tm)),
        ],
        out_specs=[
            pl.BlockSpec((tm, ATTN_WIDTH), lambda i: (i, 0)),
            pl.BlockSpec((tm, KV_WIDTH), lambda i: (i, 0)),
            pl.BlockSpec((1, KV_WIDTH, tm), lambda i: (i // nt, 0, i % nt)),
            pl.BlockSpec((1, tm // DFT_N2, DFT_N2 * FOURIER_WIDTH), lambda i: (i // nt, i % nt, 0)),
        ],
        out_shape=[
            jax.ShapeDtypeStruct((t, ATTN_WIDTH), BF16),
            jax.ShapeDtypeStruct((t, KV_WIDTH), BF16),
            jax.ShapeDtypeStruct((batch, KV_WIDTH, n), BF16),
            jax.ShapeDtypeStruct((batch, n // DFT_N2, DFT_N2 * FOURIER_WIDTH), BF16),
        ],
        compiler_params=_cparams(("arbitrary",)),
        name="in_proj",
    )(x2d, g1, w_in, qg, kg, rope_c, rope_s1, rope_s2, perm)


def _attn_kernel(sb_ref, q_ref, k_ref, vt_ref, o_ref, m_ref, l_ref, acc_ref, *, nk):
    ki = pl.program_id(3)
    bounded = sb_ref[0] == 1

    @pl.when(ki == 0)
    def _():
        m_ref[...] = jnp.full(m_ref.shape, NEG_BIG, F32)
        l_ref[...] = jnp.zeros(l_ref.shape, F32)
        acc_ref[...] = jnp.zeros(acc_ref.shape, F32)

    def scores(g):
        q = q_ref[0, :, g * HEAD_DIM:(g + 1) * HEAD_DIM]
        return lax.dot_general(k_ref[0], q, (((1,), (1,)), ((), ())), preferred_element_type=F32)

    @pl.when(bounded)
    def _():
        shift = sb_ref[1].astype(F32)
        for g in range(Q_PER_KV):
            p = jnp.exp2(scores(g) - shift)
            l_ref[g] = l_ref[g] + jnp.sum(p, axis=0, keepdims=True)
            acc_ref[g] = acc_ref[g] + jnp.dot(vt_ref[0], p.astype(BF16), preferred_element_type=F32)

    @pl.when(jnp.logical_not(bounded))
    def _():
        for g in range(Q_PER_KV):
            s = scores(g)
            m_prev = m_ref[g]
            m_new = jnp.maximum(m_prev, jnp.max(s, axis=0, keepdims=True))
            alpha = jnp.exp2(m_prev - m_new)
            p = jnp.exp2(s - m_new)
            l_ref[g] = alpha * l_ref[g] + jnp.sum(p, axis=0, keepdims=True)
            acc_ref[g] = alpha * acc_ref[g] + jnp.dot(vt_ref[0], p.astype(BF16),
                                                      preferred_element_type=F32)
            m_ref[g] = m_new

    @pl.when(ki == nk - 1)
    def _():
        for g in range(Q_PER_KV):
            o = acc_ref[g] * (1.0 / l_ref[g])
            o_ref[0, :, g * HEAD_DIM:(g + 1) * HEAD_DIM] = o.T.astype(BF16)


def _score_bound(qg, kg):
    ub = (HEAD_DIM * jnp.max(jnp.abs(qg)) * jnp.max(jnp.abs(kg))
          * (LOG2E / math.sqrt(HEAD_DIM)) * (1.0 + 1e-3))
    ok = ub <= SCORE_BOUND_MAX
    shift = jnp.where(ok, jnp.ceil(ub) - SCORE_HEADROOM, 0.0)
    return jnp.stack([ok.astype(jnp.int32), shift.astype(jnp.int32)])


def _attention(q, k, vt, score_bound):
    batch, n, _ = q.shape
    tq = tk = min(ATTN_TILE, n)
    gw = Q_PER_KV * HEAD_DIM
    return pl.pallas_call(
        functools.partial(_attn_kernel, nk=n // tk),
        grid_spec=pltpu.PrefetchScalarGridSpec(
            num_scalar_prefetch=1,
            grid=(batch, N_KV_HEADS, n // tq, n // tk),
            in_specs=[
                pl.BlockSpec((1, tq, gw), lambda b, h, qi, ki, sb: (b, qi, h)),
                pl.BlockSpec((1, tk, HEAD_DIM), lambda b, h, qi, ki, sb: (b, ki, h)),
                pl.BlockSpec((1, HEAD_DIM, tk), lambda b, h, qi, ki, sb: (b, h, ki)),
            ],
            out_specs=pl.BlockSpec((1, tq, gw), lambda b, h, qi, ki, sb: (b, qi, h)),
            scratch_shapes=[
                pltpu.VMEM((Q_PER_KV, 1, tq), F32),
                pltpu.VMEM((Q_PER_KV, 1, tq), F32),
                pltpu.VMEM((Q_PER_KV, HEAD_DIM, tq), F32),
            ],
        ),
        out_shape=jax.ShapeDtypeStruct((batch, n, ATTN_WIDTH), BF16),
        compiler_params=_cparams(("arbitrary", "arbitrary", "arbitrary", "arbitrary")),
        name="attention",
    )(score_bound, q, k, vt)


def _dft1_kernel(f_ref, fc_ref, twc_ref, tws_ref, b_ref, *, n1):
    x = f_ref[0]
    a = jnp.dot(fc_ref[...].astype(BF16), x, preferred_element_type=F32)
    ar = a[:n1]
    ai = a[n1:]
    width = b_ref.shape[3]
    for t in range(b_ref.shape[1]):
        c = twc_ref[t]
        s = tws_ref[t]
        for j in range(width // LANES):
            sl = slice(t * width + j * LANES, t * width + (j + 1) * LANES)
            br = ar[:, sl] * c + ai[:, sl] * s
            bi = ai[:, sl] * c - ar[:, sl] * s
            b_ref[0, t, :, j * LANES:(j + 1) * LANES] = pltpu.pack_elementwise([br, bi], packed_dtype=BF16)


def _dft_stage3(b_ref, f3_ref, rows_ref, uv_ref):
    n2, kc, c = b_ref.shape[1:]
    n_lane_tiles = c // LANES
    for jc in range(n_lane_tiles):
        rows_ref[jc] = b_ref[0, :, :, jc * LANES:(jc + 1) * LANES].reshape(n2 * kc, LANES)
    f3 = f3_ref[...].astype(BF16)
    for j in range(kc):
        w = jnp.concatenate([rows_ref[jc, pl.ds(j, n2, stride=kc), :] for jc in range(n_lane_tiles)],
                            axis=1)
        br = pltpu.unpack_elementwise(w, index=0, packed_dtype=BF16, unpacked_dtype=F32)
        bi = pltpu.unpack_elementwise(w, index=1, packed_dtype=BF16, unpacked_dtype=F32)
        rhs = jnp.concatenate([br, bi], axis=0).astype(BF16)
        uv = jnp.dot(f3, rhs, preferred_element_type=F32)
        packed = pltpu.pack_elementwise([uv[:n2], uv[n2:]], packed_dtype=BF16)
        for jc in range(n_lane_tiles):
            uv_ref[jc, pl.ds(j, n2, stride=kc), :] = packed[:, jc * LANES:(jc + 1) * LANES]


def _dft_tables(n):
    n2 = DFT_N2
    n1 = n // n2
    scale = 1.0 / math.sqrt(n * FOURIER_GROUP_DIM)
    k1 = np.arange(n1)[:, None]
    a1 = 2.0 * np.pi * ((k1 * np.arange(n1)[None, :]) % n1) / n1
    fcat = np.concatenate([np.cos(a1), -np.sin(a1)], axis=0) * scale
    th = 2.0 * np.pi * ((np.arange(n2)[:, None] * k1.T) % n) / n
    twc = np.repeat(np.cos(th)[:, :, None], LANES, axis=2)
    tws = np.repeat(np.sin(th)[:, :, None], LANES, axis=2)
    a3 = 2.0 * np.pi * ((np.arange(n2)[:, None] * np.arange(n2)[None, :]) % n2) / n2
    c3, s3 = np.cos(a3), np.sin(a3)
    f3 = np.block([[c3, s3], [-s3, c3]])
    return jnp.asarray(fcat, F32), jnp.asarray(twc, F32), jnp.asarray(tws, F32), jnp.asarray(f3, F32)


def _dft_stage1(f2, n):
    batch = f2.shape[0]
    c = FOURIER_WIDTH
    n2 = DFT_N2
    n1 = n // n2
    g = min(n2, DFT1_ROWS // n1)
    fcat, twc, tws, f3 = _dft_tables(n)
    b = pl.pallas_call(
        functools.partial(_dft1_kernel, n1=n1),
        grid=(batch, n2 // g),
        in_specs=[
            pl.BlockSpec((1, n1, g * c), lambda b, j: (b, 0, j)),
            _const_spec((2 * n1, n1)),
            pl.BlockSpec((g, n1, LANES), lambda b, j: (j, 0, 0)),
            pl.BlockSpec((g, n1, LANES), lambda b, j: (j, 0, 0)),
        ],
        out_specs=pl.BlockSpec((1, g, n1, c), lambda b, j: (b, j, 0, 0)),
        out_shape=jax.ShapeDtypeStruct((batch, n2, n1, c), jnp.uint32),
        compiler_params=_cparams(("arbitrary", "arbitrary")),
        name="dft_stage1",
    )(f2, fcat, twc, tws)
    return b, f3


def _rms(z, g):
    ms = jnp.mean(z * z, axis=-1, keepdims=True)
    return z * lax.rsqrt(ms + EPS) * g


def _outproj_kernel(x_ref, a_ref, b_ref, f3_ref, cc_ref, sc_ref, wf_ref, ga_ref, gf_ref, wo_ref,
                    g2_ref, wr_ref, br_ref,
                    x1_ref, h2_ref, info_ref, cnt_ref, rows_ref, uv_ref):
    i = pl.program_id(0)
    n2, kc, _ = b_ref.shape[1:]
    tm = n2 * kc
    an = _rms(a_ref[0].reshape(tm, ATTN_WIDTH).astype(F32), ga_ref[...]).astype(BF16)
    mix_a = jnp.dot(an, wo_ref[0:ATTN_WIDTH, :], preferred_element_type=F32)
    _dft_stage3(b_ref, f3_ref, rows_ref, uv_ref)
    n_lane_tiles = FOURIER_WIDTH // LANES
    uvw = jnp.concatenate([uv_ref[jc] for jc in range(n_lane_tiles)], axis=1)
    u = pltpu.unpack_elementwise(uvw, index=0, packed_dtype=BF16, unpacked_dtype=F32).astype(BF16)
    v = pltpu.unpack_elementwise(uvw, index=1, packed_dtype=BF16, unpacked_dtype=F32).astype(BF16)
    cc = cc_ref[...].astype(BF16)
    sc = sc_ref[...].astype(BF16)
    parts = []
    for g in range(N_FOURIER_GROUPS):
        sl = slice(g * FOURIER_GROUP_DIM, (g + 1) * FOURIER_GROUP_DIM)
        parts.append(jnp.dot(u[:, sl], cc, preferred_element_type=F32)
                     + jnp.dot(v[:, sl], sc, preferred_element_type=F32))
    fmix = jnp.concatenate(parts, axis=-1).astype(BF16)
    four = jnp.dot(fmix, wf_ref[...], preferred_element_type=F32)
    fn = _rms(four, gf_ref[...]).astype(BF16)
    mix = mix_a + jnp.dot(fn, wo_ref[ATTN_WIDTH:D_MODEL, :], preferred_element_type=F32)
    x1 = x_ref[0].reshape(tm, D_MODEL) + mix
    x1_ref[...] = x1
    h2f = _rms(x1, g2_ref[...])
    h2 = h2f.astype(BF16)
    h2_ref[...] = pltpu.pack_elementwise([h2f[:, :D_MODEL // 2], h2f[:, D_MODEL // 2:]], packed_dtype=BF16)

    lg = jnp.dot(h2, wr_ref[...], preferred_element_type=F32) + br_ref[...]
    lane = lax.broadcasted_iota(jnp.int32, lg.shape, 1).astype(F32)
    is_g = lane < N_EXPERT_GROUPS
    gl = jnp.where(is_g, lg, NEG_BIG)
    gmax = jnp.max(gl, axis=-1, keepdims=True)
    gidx = jnp.min(jnp.where(gl == gmax, lane, float(LANES)), axis=-1, keepdims=True)
    gsum = jnp.sum(jnp.where(is_g, jnp.exp(gl - gmax), 0.0), axis=-1, keepdims=True)
    g_w = 1.0 / gsum
    lo_lane = N_EXPERT_GROUPS + EXPERTS_PER_GROUP * gidx
    in_grp = jnp.logical_and(lane >= lo_lane, lane < lo_lane + EXPERTS_PER_GROUP)
    el = jnp.where(in_grp, lg, NEG_BIG)
    m1 = jnp.max(el, axis=-1, keepdims=True)
    i1 = jnp.min(jnp.where(el == m1, lane, float(LANES)), axis=-1, keepdims=True)
    el2 = jnp.where(lane == i1, NEG_BIG, el)
    m2 = jnp.max(el2, axis=-1, keepdims=True)
    i2 = jnp.min(jnp.where(el2 == m2, lane, float(LANES)), axis=-1, keepdims=True)
    r = jnp.exp(m2 - m1)
    w1 = g_w / (1.0 + r)
    w2 = g_w * r / (1.0 + r)
    first_lo = i1 < i2
    ea = jnp.minimum(i1, i2) - lo_lane
    eb = jnp.maximum(i1, i2) - lo_lane
    pair = ea * (7.0 - ea) * 0.5 + (eb - ea - 1.0)
    bucket = gidx * float(N_PAIRS) + pair
    w_lo = jnp.where(first_lo, w1, w2)
    w_hi = jnp.where(first_lo, w2, w1)
    info_ref[...] = jnp.where(lane == 0.0, bucket,
                              jnp.where(lane == 1.0, w_lo, jnp.where(lane == 2.0, w_hi, 0.0)))
    cnt = jnp.sum(jnp.where(lane == bucket, 1.0, 0.0), axis=0, keepdims=True)

    @pl.when(i == 0)
    def _():
        cnt_ref[...] = jnp.zeros(cnt_ref.shape, F32)

    cnt_ref[...] = cnt_ref[...] + cnt


def _out_proj(x, attn, b, f3, cc, sc, w_fmix, ga, gf, w_out, g2, w_router, b_router):
    batch, n, _ = x.shape
    n2 = DFT_N2
    n1 = n // n2
    kc = DFT3_ROWS
    nj = n1 // kc
    tm = n2 * kc
    assert tm == TOKEN_TILE
    t = batch * n
    row = lambda i: (i, 0)
    grp = lambda i: (i // nj, 0, i % nj, 0)
    n_lane_tiles = FOURIER_WIDTH // LANES
    return pl.pallas_call(
        _outproj_kernel,
        grid=(t // tm,),
        in_specs=[
            pl.BlockSpec((1, n2, kc, D_MODEL), grp),
            pl.BlockSpec((1, n2, kc, ATTN_WIDTH), grp),
            pl.BlockSpec((1, n2, kc, FOURIER_WIDTH), grp),
            _const_spec((2 * n2, 2 * n2)),
            _const_spec((FOURIER_GROUP_DIM, FOURIER_GROUP_DIM)),
            _const_spec((FOURIER_GROUP_DIM, FOURIER_GROUP_DIM)),
            _const_spec((FOURIER_WIDTH, FOURIER_WIDTH)),
            _const_spec((1, ATTN_WIDTH)),
            _const_spec((1, FOURIER_WIDTH)),
            _const_spec((D_MODEL, D_MODEL)),
            _const_spec((1, D_MODEL)),
            _const_spec((D_MODEL, LANES)),
            _const_spec((1, LANES)),
        ],
        out_specs=[
            pl.BlockSpec((tm, D_MODEL), row),
            pl.BlockSpec((tm, D_MODEL // 2), row),
            pl.BlockSpec((tm, LANES), row),
            pl.BlockSpec((8, LANES), lambda i: (0, 0)),
        ],
        out_shape=[
            jax.ShapeDtypeStruct((t, D_MODEL), F32),
            jax.ShapeDtypeStruct((t, D_MODEL // 2), jnp.uint32),
            jax.ShapeDtypeStruct((t, LANES), F32),
            jax.ShapeDtypeStruct((8, LANES), F32),
        ],
        scratch_shapes=[
            pltpu.VMEM((n_lane_tiles, tm, LANES), jnp.uint32),
            pltpu.VMEM((n_lane_tiles, tm, LANES), jnp.uint32),
        ],
        compiler_params=_cparams(("arbitrary",)),
        name="out_proj_router",
    )(x.reshape(batch, n2, n1, D_MODEL), attn.reshape(batch, n2, n1, ATTN_WIDTH), b, f3,
      cc, sc, w_fmix, ga, gf, w_out, g2, w_router, b_router)


def _rank_kernel(info_ref, start_ref, pos_ref, base_ref):
    i = pl.program_id(0)
    tm = info_ref.shape[0]

    @pl.when(i == 0)
    def _():
        base_ref[...] = start_ref[...]

    bucket = info_ref[...].T[0:1, :]
    brow = lax.broadcasted_iota(jnp.int32, (32, tm), 0).astype(F32)
    onehot = jnp.where(brow == bucket, 1.0, 0.0)
    oh = onehot.astype(BF16)
    src = lax.broadcasted_iota(jnp.int32, (tm, tm), 0)
    dst = lax.broadcasted_iota(jnp.int32, (tm, tm), 1)
    upper = jnp.where(src <= dst, 1.0, 0.0).astype(BF16)
    cum = jnp.dot(oh, upper, preferred_element_type=F32)
    base = base_ref[...]
    base_t = jnp.concatenate([base] * (tm // LANES), axis=1)
    pos = jnp.sum(onehot * (cum - 1.0 + base_t), axis=0, keepdims=True)
    pos_ref[0] = pos.astype(jnp.int32)
    ones = jnp.ones((tm, LANES), BF16)
    base_ref[...] = base + jnp.dot(oh, ones, preferred_element_type=F32)


def _rank(info, starts):
    t = info.shape[0]
    tm = TOKEN_TILE
    return pl.pallas_call(
        _rank_kernel,
        grid=(t // tm,),
        in_specs=[
            pl.BlockSpec((tm, LANES), lambda i: (i, 0)),
            pl.BlockSpec((32, LANES), lambda i: (0, 0)),
        ],
        out_specs=pl.BlockSpec((1, 1, tm), lambda i: (i, 0, 0)),
        out_shape=jax.ShapeDtypeStruct((t // tm, 1, tm), jnp.int32),
        scratch_shapes=[pltpu.VMEM((32, LANES), F32)],
        compiler_params=_cparams(("arbitrary",)),
        name="rank",
    )(info, starts)


def _for_each_row(n_rows, fn):
    def body(c, carry):
        for u in range(ROW_DMA_UNROLL):
            fn(c * ROW_DMA_UNROLL + u)
        return carry

    lax.fori_loop(0, n_rows // ROW_DMA_UNROLL, body, 0)


def _scatter_kernel(pos_ref, h_ref, xs_in_ref, xs_ref, sem):
    del xs_in_ref
    tm = h_ref.shape[0]

    def row_copy(r):
        return pltpu.make_async_copy(h_ref.at[pl.ds(r, 1)], xs_ref.at[pl.ds(pos_ref[0, 0, r], 1)], sem)

    _for_each_row(tm, lambda r: row_copy(r).start())
    _for_each_row(tm, lambda r: row_copy(r).wait())


def _scatter(pos, h2, xs):
    t, w = h2.shape
    tm = TOKEN_TILE
    return pl.pallas_call(
        _scatter_kernel,
        grid=(t // tm,),
        in_specs=[
            pl.BlockSpec((1, 1, tm), lambda i: (i, 0, 0), memory_space=pltpu.SMEM),
            pl.BlockSpec((tm, w), lambda i: (i, 0)),
            pl.BlockSpec(memory_space=pl.ANY),
        ],
        out_specs=pl.BlockSpec(memory_space=pl.ANY),
        out_shape=jax.ShapeDtypeStruct(xs.shape, xs.dtype),
        scratch_shapes=[pltpu.SemaphoreType.DMA],
        input_output_aliases={2: 0},
        compiler_params=_cparams(("arbitrary",)),
        name="scatter_rows",
    )(pos, h2, xs)


def _expert_kernel(lo_ref, hi_ref, blk_ref, valid_ref, x_ref,
                   wg0_ref, wu0_ref, wd0_ref, wg1_ref, wu1_ref, wd1_ref, o_ref):
    del lo_ref, hi_ref, blk_ref
    i = pl.program_id(0)

    @pl.when(valid_ref[i] == 1)
    def _():
        xw = x_ref[...]
        x = jnp.concatenate(
            [pltpu.unpack_elementwise(xw, index=0, packed_dtype=BF16, unpacked_dtype=F32),
             pltpu.unpack_elementwise(xw, index=1, packed_dtype=BF16, unpacked_dtype=F32)],
            axis=1).astype(BF16)

        def mlp(wg_ref, wu_ref, wd_ref):
            a = jnp.dot(x, wg_ref[0], preferred_element_type=F32)
            u = jnp.dot(x, wu_ref[0], preferred_element_type=F32)
            h = (a * (1.0 / (1.0 + jnp.exp(-a))) * u).astype(BF16)
            return jnp.dot(h, wd_ref[0], preferred_element_type=F32)

        e_lo = mlp(wg0_ref, wu0_ref, wd0_ref)
        e_hi = mlp(wg1_ref, wu1_ref, wd1_ref)
        o_ref[...] = pltpu.pack_elementwise([e_lo, e_hi], packed_dtype=BF16)

    @pl.when(valid_ref[i] == 0)
    def _():
        zero = jnp.zeros(o_ref.shape, F32)
        o_ref[...] = pltpu.pack_elementwise([zero, zero], packed_dtype=BF16)


def _experts(lo, hi, blk, valid, xs, wg, wu, wd):
    n_tiles = lo.shape[0]
    r = EXPERT_TILE
    x_map = lambda i, lo, hi, blk, valid: (blk[i], 0)
    lo_map = lambda i, lo, hi, blk, valid: (lo[i], 0, 0)
    hi_map = lambda i, lo, hi, blk, valid: (hi[i], 0, 0)
    one = pl.Buffered(1)
    w_up = (1, D_MODEL, EXPERT_FF)
    w_dn = (1, EXPERT_FF, D_MODEL)
    return pl.pallas_call(
        _expert_kernel,
        grid_spec=pltpu.PrefetchScalarGridSpec(
            num_scalar_prefetch=4,
            grid=(n_tiles,),
            in_specs=[
                pl.BlockSpec((r, D_MODEL // 2), x_map),
                pl.BlockSpec(w_up, lo_map, pipeline_mode=one),
                pl.BlockSpec(w_up, lo_map, pipeline_mode=one),
                pl.BlockSpec(w_dn, lo_map, pipeline_mode=one),
                pl.BlockSpec(w_up, hi_map),
                pl.BlockSpec(w_up, hi_map),
                pl.BlockSpec(w_dn, hi_map),
            ],
            out_specs=pl.BlockSpec((r, D_MODEL), lambda i, lo, hi, blk, valid: (i, 0)),
        ),
        out_shape=jax.ShapeDtypeStruct((n_tiles * r, D_MODEL), jnp.uint32),
        compiler_params=_cparams(("arbitrary",)),
        name="experts",
    )(lo, hi, blk, valid, xs, wg, wu, wd, wg, wu, wd)


def _combine_kernel(pos_ref, nxt_ref, x1_ref, info_ref, eo_ref, y_ref, gbuf, sems, *, n_steps):
    i = pl.program_id(0)
    tm = x1_ref.shape[0]
    slot = lax.rem(i, 2)

    def row_copy(p_ref, s, r):
        return pltpu.make_async_copy(eo_ref.at[pl.ds(p_ref[0, 0, r], 1)], gbuf.at[s, pl.ds(r, 1)],
                                     sems.at[s])

    @pl.when(i == 0)
    def _():
        _for_each_row(tm, lambda r: row_copy(pos_ref, 0, r).start())

    @pl.when(i + 1 < n_steps)
    def _():
        _for_each_row(tm, lambda r: row_copy(nxt_ref, 1 - slot, r).start())

    _for_each_row(tm, lambda r: row_copy(pos_ref, slot, r).wait())

    g = gbuf[slot]
    e_lo = pltpu.unpack_elementwise(g, index=0, packed_dtype=BF16, unpacked_dtype=F32)
    e_hi = pltpu.unpack_elementwise(g, index=1, packed_dtype=BF16, unpacked_dtype=F32)
    info = info_ref[...]
    y = x1_ref[...] + info[:, 1:2] * e_lo + info[:, 2:3] * e_hi
    y_ref[0] = y.reshape(y_ref.shape[1:])


def _combine(pos, x1, info, eo, batch, n):
    n2 = DFT_N2
    n1 = n // n2
    kc = DFT3_ROWS
    nj = n1 // kc
    tm = n2 * kc
    n_steps = batch * nj
    return pl.pallas_call(
        functools.partial(_combine_kernel, n_steps=n_steps),
        grid=(n_steps,),
        in_specs=[
            pl.BlockSpec((1, 1, tm), lambda i: (i, 0, 0), memory_space=pltpu.SMEM),
            pl.BlockSpec((1, 1, tm), lambda i: (jnp.minimum(i + 1, n_steps - 1), 0, 0),
                         memory_space=pltpu.SMEM),
            pl.BlockSpec((tm, D_MODEL), lambda i: (i, 0)),
            pl.BlockSpec((tm, LANES), lambda i: (i, 0)),
            pl.BlockSpec(memory_space=pl.ANY),
        ],
        out_specs=pl.BlockSpec((1, n2, kc, D_MODEL), lambda i: (i // nj, 0, i % nj, 0)),
        out_shape=jax.ShapeDtypeStruct((batch, n2, n1, D_MODEL), F32),
        scratch_shapes=[pltpu.VMEM((2, tm, D_MODEL), jnp.uint32), pltpu.SemaphoreType.DMA((2,))],
        compiler_params=_cparams(("arbitrary",)),
        name="combine",
    )(pos, pos, x1, info, eo)


def _rope_tables(n):
    rows = n // GRID_W
    row_id = np.repeat(np.arange(rows, dtype=np.float64), GRID_W)
    col_id = np.tile(np.arange(GRID_W, dtype=np.float64), rows)
    inv_freq = np.exp(-math.log(ROPE_THETA) * 2.0 * np.arange(ROPE_FREQS, dtype=np.float64) / (2 * ROPE_FREQS))
    ar = row_id[:, None] * inv_freq
    ac = col_id[:, None] * inv_freq
    cr, sr, ccol, scol = np.cos(ar), np.sin(ar), np.cos(ac), np.sin(ac)
    zero = np.zeros_like(sr)
    c = np.concatenate([cr, cr, ccol, ccol], axis=1)
    s1 = np.concatenate([-sr, zero, -scol, zero], axis=1)
    s2 = np.concatenate([zero, sr, zero, scol], axis=1)
    return jnp.asarray(c, F32), jnp.asarray(s1, F32), jnp.asarray(s2, F32)


def _channel_dft_tables():
    g = FOURIER_GROUP_DIM
    a = 2.0 * np.pi * ((np.arange(g)[:, None] * np.arange(g)[None, :]) % g) / g
    return jnp.asarray(np.cos(a), F32), jnp.asarray(np.sin(a), F32)


_PAIR_LO = np.array([0, 0, 0, 1, 1, 2], np.int32)
_PAIR_HI = np.array([1, 2, 3, 2, 3, 3], np.int32)


def kernel(x_prompt, x_sample, norm1_g, w_in, q_norm_g, k_norm_g, w_fmix, attn_out_g, fourier_out_g,
           w_out, norm2_g, w_router_group, b_router_group, w_router_expert, b_router_expert,
           w_gate, w_up, w_down):
    l = 0
    g1 = norm1_g[l][None, :]
    g2 = norm2_g[l][None, :]
    qg = q_norm_g[l][None, :]
    kg = k_norm_g[l][None, :]
    ga = attn_out_g[l][None, :]
    gf = fourier_out_g[l][None, :]
    w_in_b = w_in[l].astype(BF16)
    w_fmix_b = w_fmix[l].astype(BF16)
    w_out_b = w_out[l].astype(BF16)
    wg_b = w_gate[l].astype(BF16)
    wu_b = w_up[l].astype(BF16)
    wd_b = w_down[l].astype(BF16)
    n_r = N_EXPERT_GROUPS + N_EXPERTS
    w_router = jnp.zeros((D_MODEL, LANES), F32)
    w_router = w_router.at[:, :N_EXPERT_GROUPS].set(w_router_group[l])
    w_router = w_router.at[:, N_EXPERT_GROUPS:n_r].set(w_router_expert[l]).astype(BF16)
    b_router = jnp.zeros((1, LANES), F32)
    b_router = b_router.at[0, :N_EXPERT_GROUPS].set(b_router_group[l])
    b_router = b_router.at[0, N_EXPERT_GROUPS:n_r].set(b_router_expert[l])
    cc, sc = _channel_dft_tables()
    score_bound = _score_bound(qg, kg)

    seqs = []
    for x in (x_prompt, x_sample):
        batch, n, _ = x.shape
        x2d = x.reshape(batch * n, D_MODEL)
        rope_c, rope_s1, rope_s2 = _rope_tables(n)
        q, k, vt, f2 = _in_proj(x2d, n, g1, w_in_b, qg, kg, rope_c, rope_s1, rope_s2)
        attn = _attention(q.reshape(batch, n, ATTN_WIDTH), k.reshape(batch, n, KV_WIDTH), vt, score_bound)
        b, f3 = _dft_stage1(f2, n)
        x1, h2, info, cnt = _out_proj(x, attn, b, f3, cc, sc, w_fmix_b, ga, gf, w_out_b, g2, w_router, b_router)
        seqs.append((x.shape, x1, h2, info, cnt))

    r = EXPERT_TILE
    t_all = sum(s[1].shape[0] for s in seqs)
    n_tiles = t_all // r + N_BUCKETS
    counts = sum(s[4][0, :N_BUCKETS] for s in seqs).astype(jnp.int32)
    tiles_b = (counts + (r - 1)) // r
    tile_end = jnp.cumsum(tiles_b)
    tile_start = tile_end - tiles_b
    total_tiles = tile_end[-1]
    starts = jnp.zeros((32,), F32).at[:N_BUCKETS].set((tile_start * r).astype(F32))
    starts = jnp.broadcast_to(starts[:, None], (32, LANES))
    tile_id = jnp.arange(n_tiles, dtype=jnp.int32)
    blk = jnp.minimum(tile_id, total_tiles - 1)
    tile_bucket = jnp.sum((tile_end[None, :] <= blk[:, None]).astype(jnp.int32), axis=1)
    grp = tile_bucket // N_PAIRS
    pair = tile_bucket % N_PAIRS
    lo = grp * EXPERTS_PER_GROUP + jnp.asarray(_PAIR_LO)[pair]
    hi = grp * EXPERTS_PER_GROUP + jnp.asarray(_PAIR_HI)[pair]
    valid = (tile_id < total_tiles).astype(jnp.int32)

    info_all = jnp.concatenate([s[3] for s in seqs], axis=0)
    pos = _rank(info_all, starts)

    xs = jnp.zeros((n_tiles * r, D_MODEL // 2), jnp.uint32)
    off = 0
    for _, _, h2, _, _ in seqs:
        nt = h2.shape[0] // TOKEN_TILE
        xs = _scatter(pos[off:off + nt], h2, xs)
        off += nt
    eo = _experts(lo, hi, blk, valid, xs, wg_b, wu_b, wd_b)

    outs = []
    off = 0
    for shape, x1, _, info, _ in seqs:
        nt = x1.shape[0] // TOKEN_TILE
        y = _combine(pos[off:off + nt], x1, info, eo, shape[0], shape[1])
        outs.append(y.reshape(shape))
        off += nt
    return tuple(outs)
```

```python
import functools
import math

import numpy as np
import jax
import jax.numpy as jnp
from jax import lax
from jax.experimental import pallas as pl
from jax.experimental.pallas import tpu as pltpu

F32 = jnp.float32
BF16 = jnp.bfloat16

EPS = 1e-6
D_MODEL = 2048
HEAD_DIM = 128
N_HEADS = 8
N_KV_HEADS = 2
Q_PER_KV = N_HEADS // N_KV_HEADS
ATTN_WIDTH = N_HEADS * HEAD_DIM
KV_WIDTH = N_KV_HEADS * HEAD_DIM
FOURIER_WIDTH = D_MODEL - ATTN_WIDTH
N_FOURIER_GROUPS = 4
FOURIER_GROUP_DIM = FOURIER_WIDTH // N_FOURIER_GROUPS
IN_WIDTH = ATTN_WIDTH + 2 * KV_WIDTH + FOURIER_WIDTH
GRID_W = 64
ROPE_THETA = 10000.0
ROPE_FREQS = HEAD_DIM // 4
N_EXPERT_GROUPS = 4
EXPERTS_PER_GROUP = 4
N_EXPERTS = N_EXPERT_GROUPS * EXPERTS_PER_GROUP
EXPERT_FF = D_MODEL // 2
N_PAIRS = 6
N_BUCKETS = N_EXPERT_GROUPS * N_PAIRS

LANES = 128
TOKEN_TILE = 512
ATTN_TILE = 1024
EXPERT_TILE = 256
DFT_N2 = 32
DFT1_ROWS = 512
DFT3_ROWS = 16
ROW_DMA_UNROLL = 8
VMEM_LIMIT = 56 * 1024 * 1024
NEG_BIG = -1e30
LOG2E = 1.4426950408889634
SCORE_HEADROOM = 64
SCORE_BOUND_MAX = 87.0


def _cparams(sem, vmem=VMEM_LIMIT):
    return pltpu.CompilerParams(dimension_semantics=sem, vmem_limit_bytes=vmem)


def _const_spec(shape):
    nd = len(shape)
    return pl.BlockSpec(shape, lambda *_: (0,) * nd, pipeline_mode=pl.Buffered(1))


def _inproj_kernel(x_ref, g1_ref, w_ref, qg_ref, kg_ref, c_ref, s1_ref, s2_ref, perm_ref,
                   q_ref, k_ref, vt_ref, f_ref, *, q_scale):
    x = x_ref[...]
    r = lax.rsqrt(jnp.mean(x * x, axis=-1, keepdims=True) + EPS)
    h = (x * g1_ref[...]).astype(BF16)
    c = c_ref[...]
    s1 = s1_ref[...]
    s2 = s2_ref[...]

    def norm_rope(z, g):
        zms = jnp.mean(z * z, axis=-1, keepdims=True)
        zn = z * lax.rsqrt(zms + EPS) * g
        return zn * c + pltpu.roll(zn, 96, 1) * s1 + pltpu.roll(zn, 32, 1) * s2

    qg = qg_ref[...]
    kg = kg_ref[...]
    pq = r * jnp.dot(h, w_ref[:, 0:ATTN_WIDTH], preferred_element_type=F32)
    for hh in range(N_HEADS):
        sl = slice(hh * HEAD_DIM, (hh + 1) * HEAD_DIM)
        q_ref[:, sl] = (norm_rope(pq[:, sl], qg) * q_scale).astype(BF16)
    pk = r * jnp.dot(h, w_ref[:, ATTN_WIDTH:ATTN_WIDTH + KV_WIDTH], preferred_element_type=F32)
    for hh in range(N_KV_HEADS):
        sl = slice(hh * HEAD_DIM, (hh + 1) * HEAD_DIM)
        k_ref[:, sl] = norm_rope(pk[:, sl], kg).astype(BF16)
    pv = r * jnp.dot(h, w_ref[:, ATTN_WIDTH + KV_WIDTH:ATTN_WIDTH + 2 * KV_WIDTH],
                     preferred_element_type=F32)
    vt_ref[0] = pv.T.astype(BF16)
    pf = r * jnp.dot(h, w_ref[:, ATTN_WIDTH + 2 * KV_WIDTH:IN_WIDTH], preferred_element_type=F32)
    pfp = jnp.dot(perm_ref[...], pf.astype(BF16), preferred_element_type=F32).astype(BF16)
    rows = pfp.shape[0] // DFT_N2
    for t2 in range(DFT_N2):
        f_ref[0, :, t2 * FOURIER_WIDTH:(t2 + 1) * FOURIER_WIDTH] = pfp[t2 * rows:(t2 + 1) * rows, :]


def _in_proj(x2d, n, g1, w_in, qg, kg, rope_c, rope_s1, rope_s2):
    t = x2d.shape[0]
    tm = TOKEN_TILE
    nt = n // tm
    batch = t // n
    q_scale = LOG2E / math.sqrt(HEAD_DIM)
    r_out = np.arange(tm)
    perm = np.zeros((tm, tm), np.float32)
    perm[r_out, (r_out % (tm // DFT_N2)) * DFT_N2 + r_out // (tm // DFT_N2)] = 1.0
    perm = jnp.asarray(perm, BF16)
    return pl.pallas_call(
        functools.partial(_inproj_kernel, q_scale=q_scale),
        grid=(t // tm,),
        in_specs=[
            pl.BlockSpec((tm, D_MODEL), lambda i: (i, 0)),
            _const_spec((1, D_MODEL)),
            _const_spec((D_MODEL, IN_WIDTH)),
            _const_spec((1, HEAD_DIM)),
            _const_spec((1, HEAD_DIM)),
            pl.BlockSpec((tm, HEAD_DIM), lambda i: (i % nt, 0)),
            pl.BlockSpec((tm, HEAD_DIM), lambda i: (i % nt, 0)),
            pl.BlockSpec((tm, HEAD_DIM), lambda i: (i % nt, 0)),
            _const_spec((tm, tm)),
        ],
        out_specs=[
            pl.BlockSpec((tm, ATTN_WIDTH), lambda i: (i, 0)),
            pl.BlockSpec((tm, KV_WIDTH), lambda i: (i, 0)),
            pl.BlockSpec((1, KV_WIDTH, tm), lambda i: (i // nt, 0, i % nt)),
            pl.BlockSpec((1, tm // DFT_N2, DFT_N2 * FOURIER_WIDTH), lambda i: (i // nt, i % nt, 0)),
        ],
        out_shape=[
            jax.ShapeDtypeStruct((t, ATTN_WIDTH), BF16),
            jax.ShapeDtypeStruct((t, KV_WIDTH), BF16),
            jax.ShapeDtypeStruct((batch, KV_WIDTH, n), BF16),
            jax.ShapeDtypeStruct((batch, n // DFT_N2, DFT_N2 * FOURIER_WIDTH), BF16),
        ],
        compiler_params=_cparams(("arbitrary",)),
        name="in_proj",
    )(x2d, g1, w_in, qg, kg, rope_c, rope_s1, rope_s2, perm)


def _attn_kernel(sb_ref, q_ref, k_ref, vt_ref, o_ref, m_ref, l_ref, acc_ref, *, nk):
    ki = pl.program_id(3)
    bounded = sb_ref[0] == 1

    @pl.when(ki == 0)
    def _():
        m_ref[...] = jnp.full(m_ref.shape, NEG_BIG, F32)
        l_ref[...] = jnp.zeros(l_ref.shape, F32)
        acc_ref[...] = jnp.zeros(acc_ref.shape, F32)

    def scores(g):
        q = q_ref[0, :, g * HEAD_DIM:(g + 1) * HEAD_DIM]
        return lax.dot_general(k_ref[0], q, (((1,), (1,)), ((), ())), preferred_element_type=F32)

    @pl.when(bounded)
    def _():
        shift = sb_ref[1].astype(F32)
        for g in range(Q_PER_KV):
            p = jnp.exp2(scores(g) - shift)
            l_ref[g] = l_ref[g] + jnp.sum(p, axis=0, keepdims=True)
            acc_ref[g] = acc_ref[g] + jnp.dot(vt_ref[0], p.astype(BF16), preferred_element_type=F32)

    @pl.when(jnp.logical_not(bounded))
    def _():
        for g in range(Q_PER_KV):
            s = scores(g)
            m_prev = m_ref[g]
            m_new = jnp.maximum(m_prev, jnp.max(s, axis=0, keepdims=True))
            alpha = jnp.exp2(m_prev - m_new)
            p = jnp.exp2(s - m_new)
            l_ref[g] = alpha * l_ref[g] + jnp.sum(p, axis=0, keepdims=True)
            acc_ref[g] = alpha * acc_ref[g] + jnp.dot(vt_ref[0], p.astype(BF16),
                                                      preferred_element_type=F32)
            m_ref[g] = m_new

    @pl.when(ki == nk - 1)
    def _():
        for g in range(Q_PER_KV):
            o = acc_ref[g] * (1.0 / l_ref[g])
            o_ref[0, :, g * HEAD_DIM:(g + 1) * HEAD_DIM] = o.T.astype(BF16)


def _score_bound(qg, kg):
    ub = (HEAD_DIM * jnp.max(jnp.abs(qg)) * jnp.max(jnp.abs(kg))
          * (LOG2E / math.sqrt(HEAD_DIM)) * (1.0 + 1e-3))
    ok = ub <= SCORE_BOUND_MAX
    shift = jnp.where(ok, jnp.ceil(ub) - SCORE_HEADROOM, 0.0)
    return jnp.stack([ok.astype(jnp.int32), shift.astype(jnp.int32)])


def _attention(q, k, vt, score_bound):
    batch, n, _ = q.shape
    tq = tk = min(ATTN_TILE, n)
    gw = Q_PER_KV * HEAD_DIM
    return pl.pallas_call(
        functools.partial(_attn_kernel, nk=n // tk),
        grid_spec=pltpu.PrefetchScalarGridSpec(
            num_scalar_prefetch=1,
            grid=(batch, N_KV_HEADS, n // tq, n // tk),
            in_specs=[
                pl.BlockSpec((1, tq, gw), lambda b, h, qi, ki, sb: (b, qi, h)),
                pl.BlockSpec((1, tk, HEAD_DIM), lambda b, h, qi, ki, sb: (b, ki, h)),
                pl.BlockSpec((1, HEAD_DIM, tk), lambda b, h, qi, ki, sb: (b, h, ki)),
            ],
            out_specs=pl.BlockSpec((1, tq, gw), lambda b, h, qi, ki, sb: (b, qi, h)),
            scratch_shapes=[
                pltpu.VMEM((Q_PER_KV, 1, tq), F32),
                pltpu.VMEM((Q_PER_KV, 1, tq), F32),
                pltpu.VMEM((Q_PER_KV, HEAD_DIM, tq), F32),
            ],
        ),
        out_shape=jax.ShapeDtypeStruct((batch, n, ATTN_WIDTH), BF16),
        compiler_params=_cparams(("arbitrary", "arbitrary", "arbitrary", "arbitrary")),
        name="attention",
    )(score_bound, q, k, vt)


def _dft1_kernel(f_ref, fc_ref, twc_ref, tws_ref, b_ref, *, n1):
    x = f_ref[0]
    a = jnp.dot(fc_ref[...].astype(BF16), x, preferred_element_type=F32)
    ar = a[:n1]
    ai = a[n1:]
    width = b_ref.shape[3]
    for t in range(b_ref.shape[1]):
        c = twc_ref[t]
        s = tws_ref[t]
        for j in range(width // LANES):
            sl = slice(t * width + j * LANES, t * width + (j + 1) * LANES)
            br = ar[:, sl] * c + ai[:, sl] * s
            bi = ai[:, sl] * c - ar[:, sl] * s
            b_ref[0, t, :, j * LANES:(j + 1) * LANES] = pltpu.pack_elementwise([br, bi], packed_dtype=BF16)


def _dft_stage3(b_ref, f3_ref, rows_ref, uv_ref):
    n2, kc, c = b_ref.shape[1:]
    n_lane_tiles = c // LANES
    for jc in range(n_lane_tiles):
        rows_ref[jc] = b_ref[0, :, :, jc * LANES:(jc + 1) * LANES].reshape(n2 * kc, LANES)
    f3 = f3_ref[...].astype(BF16)
    for j in range(kc):
        w = jnp.concatenate([rows_ref[jc, pl.ds(j, n2, stride=kc), :] for jc in range(n_lane_tiles)],
                            axis=1)
        br = pltpu.unpack_elementwise(w, index=0, packed_dtype=BF16, unpacked_dtype=F32)
        bi = pltpu.unpack_elementwise(w, index=1, packed_dtype=BF16, unpacked_dtype=F32)
        rhs = jnp.concatenate([br, bi], axis=0).astype(BF16)
        uv = jnp.dot(f3, rhs, preferred_element_type=F32)
        packed = pltpu.pack_elementwise([uv[:n2], uv[n2:]], packed_dtype=BF16)
        for jc in range(n_lane_tiles):
            uv_ref[jc, pl.ds(j, n2, stride=kc), :] = packed[:, jc * LANES:(jc + 1) * LANES]


def _dft_tables(n):
    n2 = DFT_N2
    n1 = n // n2
    scale = 1.0 / math.sqrt(n * FOURIER_GROUP_DIM)
    k1 = np.arange(n1)[:, None]
    a1 = 2.0 * np.pi * ((k1 * np.arange(n1)[None, :]) % n1) / n1
    fcat = np.concatenate([np.cos(a1), -np.sin(a1)], axis=0) * scale
    th = 2.0 * np.pi * ((np.arange(n2)[:, None] * k1.T) % n) / n
    twc = np.repeat(np.cos(th)[:, :, None], LANES, axis=2)
    tws = np.repeat(np.sin(th)[:, :, None], LANES, axis=2)
    a3 = 2.0 * np.pi * ((np.arange(n2)[:, None] * np.arange(n2)[None, :]) % n2) / n2
    c3, s3 = np.cos(a3), np.sin(a3)
    f3 = np.block([[c3, s3], [-s3, c3]])
    return jnp.asarray(fcat, F32), jnp.asarray(twc, F32), jnp.asarray(tws, F32), jnp.asarray(f3, F32)


def _dft_stage1(f2, n):
    batch = f2.shape[0]
    c = FOURIER_WIDTH
    n2 = DFT_N2
    n1 = n // n2
    g = min(n2, DFT1_ROWS // n1)
    fcat, twc, tws, f3 = _dft_tables(n)
    b = pl.pallas_call(
        functools.partial(_dft1_kernel, n1=n1),
        grid=(batch, n2 // g),
        in_specs=[
            pl.BlockSpec((1, n1, g * c), lambda b, j: (b, 0, j)),
            _const_spec((2 * n1, n1)),
            pl.BlockSpec((g, n1, LANES), lambda b, j: (j, 0, 0)),
            pl.BlockSpec((g, n1, LANES), lambda b, j: (j, 0, 0)),
        ],
        out_specs=pl.BlockSpec((1, g, n1, c), lambda b, j: (b, j, 0, 0)),
        out_shape=jax.ShapeDtypeStruct((batch, n2, n1, c), jnp.uint32),
        compiler_params=_cparams(("arbitrary", "arbitrary")),
        name="dft_stage1",
    )(f2, fcat, twc, tws)
    return b, f3


def _rms(z, g):
    ms = jnp.mean(z * z, axis=-1, keepdims=True)
    return z * lax.rsqrt(ms + EPS) * g


def _outproj_kernel(x_ref, a_ref, b_ref, f3_ref, cc_ref, sc_ref, wf_ref, ga_ref, gf_ref, wo_ref,
                    g2_ref, wr_ref, br_ref,
                    x1_ref, h2_ref, info_ref, cnt_ref, rows_ref, uv_ref):
    i = pl.program_id(0)
    n2, kc, _ = b_ref.shape[1:]
    tm = n2 * kc
    attn = a_ref[0].reshape(tm, ATTN_WIDTH).astype(F32)
    r_a = lax.rsqrt(jnp.mean(attn * attn, axis=-1, keepdims=True) + EPS)
    mix_a = jnp.dot((attn * ga_ref[...]).astype(BF16), wo_ref[0:ATTN_WIDTH, :], preferred_element_type=F32)
    _dft_stage3(b_ref, f3_ref, rows_ref, uv_ref)
    n_lane_tiles = FOURIER_WIDTH // LANES
    uvw = jnp.concatenate([uv_ref[jc] for jc in range(n_lane_tiles)], axis=1)
    u = pltpu.unpack_elementwise(uvw, index=0, packed_dtype=BF16, unpacked_dtype=F32).astype(BF16)
    v = pltpu.unpack_elementwise(uvw, index=1, packed_dtype=BF16, unpacked_dtype=F32).astype(BF16)
    cc = cc_ref[...].astype(BF16)
    sc = sc_ref[...].astype(BF16)
    parts = []
    for g in range(N_FOURIER_GROUPS):
        sl = slice(g * FOURIER_GROUP_DIM, (g + 1) * FOURIER_GROUP_DIM)
        parts.append(jnp.dot(u[:, sl], cc, preferred_element_type=F32)
                     + jnp.dot(v[:, sl], sc, preferred_element_type=F32))
    fmix = jnp.concatenate(parts, axis=-1).astype(BF16)
    four = jnp.dot(fmix, wf_ref[...], preferred_element_type=F32)
    r_f = lax.rsqrt(jnp.mean(four * four, axis=-1, keepdims=True) + EPS)
    mix_f = jnp.dot((four * gf_ref[...]).astype(BF16), wo_ref[ATTN_WIDTH:D_MODEL, :],
                    preferred_element_type=F32)
    x1 = x_ref[0].reshape(tm, D_MODEL) + (r_a * mix_a + r_f * mix_f)
    x1_ref[...] = x1
    x1g = x1 * g2_ref[...]
    r_2 = lax.rsqrt(jnp.mean(x1 * x1, axis=-1, keepdims=True) + EPS)
    h2f = x1g * r_2
    h2_ref[...] = pltpu.pack_elementwise([h2f[:, :D_MODEL // 2], h2f[:, D_MODEL // 2:]], packed_dtype=BF16)

    lg = r_2 * jnp.dot(x1g.astype(BF16), wr_ref[...], preferred_element_type=F32) + br_ref[...]
    lane = lax.broadcasted_iota(jnp.int32, lg.shape, 1).astype(F32)
    is_g = lane < N_EXPERT_GROUPS
    gl = jnp.where(is_g, lg, NEG_BIG)
    gmax = jnp.max(gl, axis=-1, keepdims=True)
    gidx = jnp.min(jnp.where(gl == gmax, lane, float(LANES)), axis=-1, keepdims=True)
    gsum = jnp.sum(jnp.where(is_g, jnp.exp(gl - gmax), 0.0), axis=-1, keepdims=True)
    g_w = 1.0 / gsum
    lo_lane = N_EXPERT_GROUPS + EXPERTS_PER_GROUP * gidx
    in_grp = jnp.logical_and(lane >= lo_lane, lane < lo_lane + EXPERTS_PER_GROUP)
    el = jnp.where(in_grp, lg, NEG_BIG)
    m1 = jnp.max(el, axis=-1, keepdims=True)
    i1 = jnp.min(jnp.where(el == m1, lane, float(LANES)), axis=-1, keepdims=True)
    el2 = jnp.where(lane == i1, NEG_BIG, el)
    m2 = jnp.max(el2, axis=-1, keepdims=True)
    i2 = jnp.min(jnp.where(el2 == m2, lane, float(LANES)), axis=-1, keepdims=True)
    r = jnp.exp(m2 - m1)
    w1 = g_w / (1.0 + r)
    w2 = g_w * r / (1.0 + r)
    first_lo = i1 < i2
    ea = jnp.minimum(i1, i2) - lo_lane
    eb = jnp.maximum(i1, i2) - lo_lane
    pair = ea * (7.0 - ea) * 0.5 + (eb - ea - 1.0)
    bucket = gidx * float(N_PAIRS) + pair
    w_lo = jnp.where(first_lo, w1, w2)
    w_hi = jnp.where(first_lo, w2, w1)
    info_ref[...] = jnp.where(lane == 0.0, bucket,
                              jnp.where(lane == 1.0, w_lo, jnp.where(lane == 2.0, w_hi, 0.0)))
    cnt = jnp.sum(jnp.where(lane == bucket, 1.0, 0.0), axis=0, keepdims=True)

    @pl.when(i == 0)
    def _():
        cnt_ref[...] = jnp.zeros(cnt_ref.shape, F32)

    cnt_ref[...] = cnt_ref[...] + cnt


def _out_proj(x, attn, b, f3, cc, sc, w_fmix, ga, gf, w_out, g2, w_router, b_router):
    batch, n, _ = x.shape
    n2 = DFT_N2
    n1 = n // n2
    kc = DFT3_ROWS
    nj = n1 // kc
    tm = n2 * kc
    assert tm == TOKEN_TILE
    t = batch * n
    row = lambda i: (i, 0)
    grp = lambda i: (i // nj, 0, i % nj, 0)
    n_lane_tiles = FOURIER_WIDTH // LANES
    return pl.pallas_call(
        _outproj_kernel,
        grid=(t // tm,),
        in_specs=[
            pl.BlockSpec((1, n2, kc, D_MODEL), grp),
            pl.BlockSpec((1, n2, kc, ATTN_WIDTH), grp),
            pl.BlockSpec((1, n2, kc, FOURIER_WIDTH), grp),
            _const_spec((2 * n2, 2 * n2)),
            _const_spec((FOURIER_GROUP_DIM, FOURIER_GROUP_DIM)),
            _const_spec((FOURIER_GROUP_DIM, FOURIER_GROUP_DIM)),
            _const_spec((FOURIER_WIDTH, FOURIER_WIDTH)),
            _const_spec((1, ATTN_WIDTH)),
            _const_spec((1, FOURIER_WIDTH)),
            _const_spec((D_MODEL, D_MODEL)),
            _const_spec((1, D_MODEL)),
            _const_spec((D_MODEL, LANES)),
            _const_spec((1, LANES)),
        ],
        out_specs=[
            pl.BlockSpec((tm, D_MODEL), row),
            pl.BlockSpec((tm, D_MODEL // 2), row),
            pl.BlockSpec((tm, LANES), row),
            pl.BlockSpec((8, LANES), lambda i: (0, 0)),
        ],
        out_shape=[
            jax.ShapeDtypeStruct((t, D_MODEL), F32),
            jax.ShapeDtypeStruct((t, D_MODEL // 2), jnp.uint32),
            jax.ShapeDtypeStruct((t, LANES), F32),
            jax.ShapeDtypeStruct((8, LANES), F32),
        ],
        scratch_shapes=[
            pltpu.VMEM((n_lane_tiles, tm, LANES), jnp.uint32),
            pltpu.VMEM((n_lane_tiles, tm, LANES), jnp.uint32),
        ],
        compiler_params=_cparams(("arbitrary",)),
        name="out_proj_router",
    )(x.reshape(batch, n2, n1, D_MODEL), attn.reshape(batch, n2, n1, ATTN_WIDTH), b, f3,
      cc, sc, w_fmix, ga, gf, w_out, g2, w_router, b_router)


def _rank_kernel(info_ref, start_ref, pos_ref, base_ref):
    i = pl.program_id(0)
    tm = info_ref.shape[0]

    @pl.when(i == 0)
    def _():
        base_ref[...] = start_ref[...]

    bucket = info_ref[...].T[0:1, :]
    brow = lax.broadcasted_iota(jnp.int32, (32, tm), 0).astype(F32)
    onehot = jnp.where(brow == bucket, 1.0, 0.0)
    oh = onehot.astype(BF16)
    src = lax.broadcasted_iota(jnp.int32, (tm, tm), 0)
    dst = lax.broadcasted_iota(jnp.int32, (tm, tm), 1)
    upper = jnp.where(src <= dst, 1.0, 0.0).astype(BF16)
    cum = jnp.dot(oh, upper, preferred_element_type=F32)
    base = base_ref[...]
    base_t = jnp.concatenate([base] * (tm // LANES), axis=1)
    pos = jnp.sum(onehot * (cum - 1.0 + base_t), axis=0, keepdims=True)
    pos_ref[0] = pos.astype(jnp.int32)
    ones = jnp.ones((tm, LANES), BF16)
    base_ref[...] = base + jnp.dot(oh, ones, preferred_element_type=F32)


def _rank(info, starts):
    t = info.shape[0]
    tm = TOKEN_TILE
    return pl.pallas_call(
        _rank_kernel,
        grid=(t // tm,),
        in_specs=[
            pl.BlockSpec((tm, LANES), lambda i: (i, 0)),
            pl.BlockSpec((32, LANES), lambda i: (0, 0)),
        ],
        out_specs=pl.BlockSpec((1, 1, tm), lambda i: (i, 0, 0)),
        out_shape=jax.ShapeDtypeStruct((t // tm, 1, tm), jnp.int32),
        scratch_shapes=[pltpu.VMEM((32, LANES), F32)],
        compiler_params=_cparams(("arbitrary",)),
        name="rank",
    )(info, starts)


def _for_each_row(n_rows, fn):
    def body(c, carry):
        for u in range(ROW_DMA_UNROLL):
            fn(c * ROW_DMA_UNROLL + u)
        return carry

    lax.fori_loop(0, n_rows // ROW_DMA_UNROLL, body, 0)


def _scatter_kernel(pos_ref, h_ref, xs_in_ref, xs_ref, sem):
    del xs_in_ref
    tm = h_ref.shape[0]

    def row_copy(r):
        return pltpu.make_async_copy(h_ref.at[pl.ds(r, 1)], xs_ref.at[pl.ds(pos_ref[0, 0, r], 1)], sem)

    _for_each_row(tm, lambda r: row_copy(r).start())
    _for_each_row(tm, lambda r: row_copy(r).wait())


def _scatter(pos, h2, xs):
    t, w = h2.shape
    tm = TOKEN_TILE
    return pl.pallas_call(
        _scatter_kernel,
        grid=(t // tm,),
        in_specs=[
            pl.BlockSpec((1, 1, tm), lambda i: (i, 0, 0), memory_space=pltpu.SMEM),
            pl.BlockSpec((tm, w), lambda i: (i, 0)),
            pl.BlockSpec(memory_space=pl.ANY),
        ],
        out_specs=pl.BlockSpec(memory_space=pl.ANY),
        out_shape=jax.ShapeDtypeStruct(xs.shape, xs.dtype),
        scratch_shapes=[pltpu.SemaphoreType.DMA],
        input_output_aliases={2: 0},
        compiler_params=_cparams(("arbitrary",)),
        name="scatter_rows",
    )(pos, h2, xs)


def _expert_kernel(lo_ref, hi_ref, blk_ref, valid_ref, x_ref,
                   wg0_ref, wu0_ref, wd0_ref, wg1_ref, wu1_ref, wd1_ref, o_ref):
    del lo_ref, hi_ref, blk_ref
    i = pl.program_id(0)

    @pl.when(valid_ref[i] == 1)
    def _():
        xw = x_ref[...]
        x = jnp.concatenate(
            [pltpu.unpack_elementwise(xw, index=0, packed_dtype=BF16, unpacked_dtype=F32),
             pltpu.unpack_elementwise(xw, index=1, packed_dtype=BF16, unpacked_dtype=F32)],
            axis=1).astype(BF16)

        def mlp(wg_ref, wu_ref, wd_ref):
            a = jnp.dot(x, wg_ref[0], preferred_element_type=F32)
            u = jnp.dot(x, wu_ref[0], preferred_element_type=F32)
            h = (a * (1.0 / (1.0 + jnp.exp(-a))) * u).astype(BF16)
            return jnp.dot(h, wd_ref[0], preferred_element_type=F32)

        e_lo = mlp(wg0_ref, wu0_ref, wd0_ref)
        e_hi = mlp(wg1_ref, wu1_ref, wd1_ref)
        o_ref[...] = pltpu.pack_elementwise([e_lo, e_hi], packed_dtype=BF16)

    @pl.when(valid_ref[i] == 0)
    def _():
        zero = jnp.zeros(o_ref.shape, F32)
        o_ref[...] = pltpu.pack_elementwise([zero, zero], packed_dtype=BF16)


def _experts(lo, hi, blk, valid, xs, wg, wu, wd):
    n_tiles = lo.shape[0]
    r = EXPERT_TILE
    x_map = lambda i, lo, hi, blk, valid: (blk[i], 0)
    lo_map = lambda i, lo, hi, blk, valid: (lo[i], 0, 0)
    hi_map = lambda i, lo, hi, blk, valid: (hi[i], 0, 0)
    one = pl.Buffered(1)
    w_up = (1, D_MODEL, EXPERT_FF)
    w_dn = (1, EXPERT_FF, D_MODEL)
    return pl.pallas_call(
        _expert_kernel,
        grid_spec=pltpu.PrefetchScalarGridSpec(
            num_scalar_prefetch=4,
            grid=(n_tiles,),
            in_specs=[
                pl.BlockSpec((r, D_MODEL // 2), x_map),
                pl.BlockSpec(w_up, lo_map, pipeline_mode=one),
                pl.BlockSpec(w_up, lo_map, pipeline_mode=one),
                pl.BlockSpec(w_dn, lo_map, pipeline_mode=one),
                pl.BlockSpec(w_up, hi_map),
                pl.BlockSpec(w_up, hi_map),
                pl.BlockSpec(w_dn, hi_map),
            ],
            out_specs=pl.BlockSpec((r, D_MODEL), lambda i, lo, hi, blk, valid: (i, 0)),
        ),
        out_shape=jax.ShapeDtypeStruct((n_tiles * r, D_MODEL), jnp.uint32),
        compiler_params=_cparams(("arbitrary",)),
        name="experts",
    )(lo, hi, blk, valid, xs, wg, wu, wd, wg, wu, wd)


def _combine_kernel(pos_ref, nxt_ref, x1_ref, info_ref, eo_ref, y_ref, gbuf, sems, *, n_steps):
    i = pl.program_id(0)
    tm = x1_ref.shape[0]
    slot = lax.rem(i, 2)

    def row_copy(p_ref, s, r):
        return pltpu.make_async_copy(eo_ref.at[pl.ds(p_ref[0, 0, r], 1)], gbuf.at[s, pl.ds(r, 1)],
                                     sems.at[s])

    @pl.when(i == 0)
    def _():
        _for_each_row(tm, lambda r: row_copy(pos_ref, 0, r).start())

    @pl.when(i + 1 < n_steps)
    def _():
        _for_each_row(tm, lambda r: row_copy(nxt_ref, 1 - slot, r).start())

    _for_each_row(tm, lambda r: row_copy(pos_ref, slot, r).wait())

    g = gbuf[slot]
    e_lo = pltpu.unpack_elementwise(g, index=0, packed_dtype=BF16, unpacked_dtype=F32)
    e_hi = pltpu.unpack_elementwise(g, index=1, packed_dtype=BF16, unpacked_dtype=F32)
    info = info_ref[...]
    y = x1_ref[...] + info[:, 1:2] * e_lo + info[:, 2:3] * e_hi
    y_ref[0] = y.reshape(y_ref.shape[1:])


def _combine(pos, x1, info, eo, batch, n):
    n2 = DFT_N2
    n1 = n // n2
    kc = DFT3_ROWS
    nj = n1 // kc
    tm = n2 * kc
    n_steps = batch * nj
    return pl.pallas_call(
        functools.partial(_combine_kernel, n_steps=n_steps),
        grid=(n_steps,),
        in_specs=[
            pl.BlockSpec((1, 1, tm), lambda i: (i, 0, 0), memory_space=pltpu.SMEM),
            pl.BlockSpec((1, 1, tm), lambda i: (jnp.minimum(i + 1, n_steps - 1), 0, 0),
                         memory_space=pltpu.SMEM),
            pl.BlockSpec((tm, D_MODEL), lambda i: (i, 0)),
            pl.BlockSpec((tm, LANES), lambda i: (i, 0)),
            pl.BlockSpec(memory_space=pl.ANY),
        ],
        out_specs=pl.BlockSpec((1, n2, kc, D_MODEL), lambda i: (i // nj, 0, i % nj, 0)),
        out_shape=jax.ShapeDtypeStruct((batch, n2, n1, D_MODEL), F32),
        scratch_shapes=[pltpu.VMEM((2, tm, D_MODEL), jnp.uint32), pltpu.SemaphoreType.DMA((2,))],
        compiler_params=_cparams(("arbitrary",)),
        name="combine",
    )(pos, pos, x1, info, eo)


def _rope_tables(n):
    rows = n // GRID_W
    row_id = np.repeat(np.arange(rows, dtype=np.float64), GRID_W)
    col_id = np.tile(np.arange(GRID_W, dtype=np.float64), rows)
    inv_freq = np.exp(-math.log(ROPE_THETA) * 2.0 * np.arange(ROPE_FREQS, dtype=np.float64) / (2 * ROPE_FREQS))
    ar = row_id[:, None] * inv_freq
    ac = col_id[:, None] * inv_freq
    cr, sr, ccol, scol = np.cos(ar), np.sin(ar), np.cos(ac), np.sin(ac)
    zero = np.zeros_like(sr)
    c = np.concatenate([cr, cr, ccol, ccol], axis=1)
    s1 = np.concatenate([-sr, zero, -scol, zero], axis=1)
    s2 = np.concatenate([zero, sr, zero, scol], axis=1)
    return jnp.asarray(c, F32), jnp.asarray(s1, F32), jnp.asarray(s2, F32)


def _channel_dft_tables():
    g = FOURIER_GROUP_DIM
    a = 2.0 * np.pi * ((np.arange(g)[:, None] * np.arange(g)[None, :]) % g) / g
    return jnp.asarray(np.cos(a), F32), jnp.asarray(np.sin(a), F32)


_PAIR_LO = np.array([0, 0, 0, 1, 1, 2], np.int32)
_PAIR_HI = np.array([1, 2, 3, 2, 3, 3], np.int32)


def kernel(x_prompt, x_sample, norm1_g, w_in, q_norm_g, k_norm_g, w_fmix, attn_out_g, fourier_out_g,
           w_out, norm2_g, w_router_group, b_router_group, w_router_expert, b_router_expert,
           w_gate, w_up, w_down):
    l = 0
    g1 = norm1_g[l][None, :]
    g2 = norm2_g[l][None, :]
    qg = q_norm_g[l][None, :]
    kg = k_norm_g[l][None, :]
    ga = attn_out_g[l][None, :]
    gf = fourier_out_g[l][None, :]
    w_in_b = w_in[l].astype(BF16)
    w_fmix_b = w_fmix[l].astype(BF16)
    w_out_b = w_out[l].astype(BF16)
    wg_b = w_gate[l].astype(BF16)
    wu_b = w_up[l].astype(BF16)
    wd_b = w_down[l].astype(BF16)
    n_r = N_EXPERT_GROUPS + N_EXPERTS
    w_router = jnp.zeros((D_MODEL, LANES), F32)
    w_router = w_router.at[:, :N_EXPERT_GROUPS].set(w_router_group[l])
    w_router = w_router.at[:, N_EXPERT_GROUPS:n_r].set(w_router_expert[l]).astype(BF16)
    b_router = jnp.zeros((1, LANES), F32)
    b_router = b_router.at[0, :N_EXPERT_GROUPS].set(b_router_group[l])
    b_router = b_router.at[0, N_EXPERT_GROUPS:n_r].set(b_router_expert[l])
    cc, sc = _channel_dft_tables()
    score_bound = _score_bound(qg, kg)

    seqs = []
    for x in (x_prompt, x_sample):
        batch, n, _ = x.shape
        x2d = x.reshape(batch * n, D_MODEL)
        rope_c, rope_s1, rope_s2 = _rope_tables(n)
        q, k, vt, f2 = _in_proj(x2d, n, g1, w_in_b, qg, kg, rope_c, rope_s1, rope_s2)
        attn = _attention(q.reshape(batch, n, ATTN_WIDTH), k.reshape(batch, n, KV_WIDTH), vt, score_bound)
        b, f3 = _dft_stage1(f2, n)
        x1, h2, info, cnt = _out_proj(x, attn, b, f3, cc, sc, w_fmix_b, ga, gf, w_out_b, g2, w_router, b_router)
        seqs.append((x.shape, x1, h2, info, cnt))

    r = EXPERT_TILE
    t_all = sum(s[1].shape[0] for s in seqs)
    n_tiles = t_all // r + N_BUCKETS
    counts = sum(s[4][0, :N_BUCKETS] for s in seqs).astype(jnp.int32)
    tiles_b = (counts + (r - 1)) // r
    tile_end = jnp.cumsum(tiles_b)
    tile_start = tile_end - tiles_b
    total_tiles = tile_end[-1]
    starts = jnp.zeros((32,), F32).at[:N_BUCKETS].set((tile_start * r).astype(F32))
    starts = jnp.broadcast_to(starts[:, None], (32, LANES))
    tile_id = jnp.arange(n_tiles, dtype=jnp.int32)
    blk = jnp.minimum(tile_id, total_tiles - 1)
    tile_bucket = jnp.sum((tile_end[None, :] <= blk[:, None]).astype(jnp.int32), axis=1)
    grp = tile_bucket // N_PAIRS
    pair = tile_bucket % N_PAIRS
    lo = grp * EXPERTS_PER_GROUP + jnp.asarray(_PAIR_LO)[pair]
    hi = grp * EXPERTS_PER_GROUP + jnp.asarray(_PAIR_HI)[pair]
    valid = (tile_id < total_tiles).astype(jnp.int32)

    info_all = jnp.concatenate([s[3] for s in seqs], axis=0)
    pos = _rank(info_all, starts)

    xs = jnp.zeros((n_tiles * r, D_MODEL // 2), jnp.uint32)
    off = 0
    for _, _, h2, _, _ in seqs:
        nt = h2.shape[0] // TOKEN_TILE
        xs = _scatter(pos[off:off + nt], h2, xs)
        off += nt
    eo = _experts(lo, hi, blk, valid, xs, wg_b, wu_b, wd_b)

    outs = []
    off = 0
    for shape, x1, _, info, _ in seqs:
        nt = x1.shape[0] // TOKEN_TILE
        y = _combine(pos[off:off + nt], x1, info, eo, shape[0], shape[1])
        outs.append(y.reshape(shape))
        off += nt
    return tuple(outs)
```

```python
import functools
import math

import numpy as np
import jax
import jax.numpy as jnp
from jax import lax
from jax.experimental import pallas as pl
from jax.experimental.pallas import tpu as pltpu

F32 = jnp.float32
BF16 = jnp.bfloat16

EPS = 1e-6
D_MODEL = 2048
HEAD_DIM = 128
N_HEADS = 8
N_KV_HEADS = 2
Q_PER_KV = N_HEADS // N_KV_HEADS
ATTN_WIDTH = N_HEADS * HEAD_DIM
KV_WIDTH = N_KV_HEADS * HEAD_DIM
FOURIER_WIDTH = D_MODEL - ATTN_WIDTH
N_FOURIER_GROUPS = 4
FOURIER_GROUP_DIM = FOURIER_WIDTH // N_FOURIER_GROUPS
IN_WIDTH = ATTN_WIDTH + 2 * KV_WIDTH + FOURIER_WIDTH
GRID_W = 64
ROPE_THETA = 10000.0
ROPE_FREQS = HEAD_DIM // 4
N_EXPERT_GROUPS = 4
EXPERTS_PER_GROUP = 4
N_EXPERTS = N_EXPERT_GROUPS * EXPERTS_PER_GROUP
EXPERT_FF = D_MODEL // 2
N_PAIRS = 6
N_BUCKETS = N_EXPERT_GROUPS * N_PAIRS

LANES = 128
TOKEN_TILE = 512
ATTN_TILE = 1024
EXPERT_TILE = 256
DFT_N2 = 32
DFT1_ROWS = 512
DFT3_ROWS = 16
ROW_DMA_UNROLL = 8
VMEM_LIMIT = 56 * 1024 * 1024
NEG_BIG = -1e30
LOG2E = 1.4426950408889634
SCORE_HEADROOM = 64
SCORE_BOUND_MAX = 87.0


def _cparams(sem, vmem=VMEM_LIMIT):
    return pltpu.CompilerParams(dimension_semantics=sem, vmem_limit_bytes=vmem)


def _const_spec(shape):
    nd = len(shape)
    return pl.BlockSpec(shape, lambda *_: (0,) * nd, pipeline_mode=pl.Buffered(1))


def _inproj_kernel(x_ref, g1_ref, w_ref, qg_ref, kg_ref, c_ref, s1_ref, s2_ref, perm_ref,
                   q_ref, k_ref, vt_ref, f_ref, *, q_scale):
    x = x_ref[...]
    r = lax.rsqrt(jnp.mean(x * x, axis=-1, keepdims=True) + EPS)
    h = (x * g1_ref[...]).astype(BF16)
    c = c_ref[...]
    s1 = s1_ref[...]
    s2 = s2_ref[...]

    def norm_rope(z, g):
        zms = jnp.mean(z * z, axis=-1, keepdims=True)
        zn = z * lax.rsqrt(zms + EPS) * g
        return zn * c + pltpu.roll(zn, 96, 1) * s1 + pltpu.roll(zn, 32, 1) * s2

    qg = qg_ref[...]
    kg = kg_ref[...]
    pq = r * jnp.dot(h, w_ref[:, 0:ATTN_WIDTH], preferred_element_type=F32)
    for hh in range(N_HEADS):
        sl = slice(hh * HEAD_DIM, (hh + 1) * HEAD_DIM)
        q_ref[:, sl] = (norm_rope(pq[:, sl], qg) * q_scale).astype(BF16)
    pk = r * jnp.dot(h, w_ref[:, ATTN_WIDTH:ATTN_WIDTH + KV_WIDTH], preferred_element_type=F32)
    for hh in range(N_KV_HEADS):
        sl = slice(hh * HEAD_DIM, (hh + 1) * HEAD_DIM)
        k_ref[:, sl] = norm_rope(pk[:, sl], kg).astype(BF16)
    pv = r * jnp.dot(h, w_ref[:, ATTN_WIDTH + KV_WIDTH:ATTN_WIDTH + 2 * KV_WIDTH],
                     preferred_element_type=F32)
    vt_ref[0] = pv.T.astype(BF16)
    pf = r * jnp.dot(h, w_ref[:, ATTN_WIDTH + 2 * KV_WIDTH:IN_WIDTH], preferred_element_type=F32)
    pfp = jnp.dot(perm_ref[...], pf.astype(BF16), preferred_element_type=F32).astype(BF16)
    rows = pfp.shape[0] // DFT_N2
    for t2 in range(DFT_N2):
        f_ref[0, :, t2 * FOURIER_WIDTH:(t2 + 1) * FOURIER_WIDTH] = pfp[t2 * rows:(t2 + 1) * rows, :]


def _in_proj(x2d, n, g1, w_in, qg, kg, rope_c, rope_s1, rope_s2):
    t = x2d.shape[0]
    tm = TOKEN_TILE
    nt = n // tm
    batch = t // n
    q_scale = LOG2E / math.sqrt(HEAD_DIM)
    r_out = np.arange(tm)
    perm = np.zeros((tm, tm), np.float32)
    perm[r_out, (r_out % (tm // DFT_N2)) * DFT_N2 + r_out // (tm // DFT_N2)] = 1.0
    perm = jnp.asarray(perm, BF16)
    return pl.pallas_call(
        functools.partial(_inproj_kernel, q_scale=q_scale),
        grid=(t // tm,),
        in_specs=[
            pl.BlockSpec((tm, D_MODEL), lambda i: (i, 0)),
            _const_spec((1, D_MODEL)),
            _const_spec((D_MODEL, IN_WIDTH)),
            _const_spec((1, HEAD_DIM)),
            _const_spec((1, HEAD_DIM)),
            pl.BlockSpec((tm, HEAD_DIM), lambda i: (i % nt, 0)),
            pl.BlockSpec((tm, HEAD_DIM), lambda i: (i % nt, 0)),
            pl.BlockSpec((tm, HEAD_DIM), lambda i: (i % nt, 0)),
            _const_spec((tm, tm)),
        ],
        out_specs=[
            pl.BlockSpec((tm, ATTN_WIDTH), lambda i: (i, 0)),
            pl.BlockSpec((tm, KV_WIDTH), lambda i: (i, 0)),
            pl.BlockSpec((1, KV_WIDTH, tm), lambda i: (i // nt, 0, i % nt)),
            pl.BlockSpec((1, tm // DFT_N2, DFT_N2 * FOURIER_WIDTH), lambda i: (i // nt, i % nt, 0)),
        ],
        out_shape=[
            jax.ShapeDtypeStruct((t, ATTN_WIDTH), BF16),
            jax.ShapeDtypeStruct((t, KV_WIDTH), BF16),
            jax.ShapeDtypeStruct((batch, KV_WIDTH, n), BF16),
            jax.ShapeDtypeStruct((batch, n // DFT_N2, DFT_N2 * FOURIER_WIDTH), BF16),
        ],
        compiler_params=_cparams(("arbitrary",)),
        name="in_proj",
    )(x2d, g1, w_in, qg, kg, rope_c, rope_s1, rope_s2, perm)


def _attn_kernel(sb_ref, q_ref, k_ref, vt_ref, o_ref, m_ref, l_ref, acc_ref, *, nk):
    ki = pl.program_id(3)
    bounded = sb_ref[0] == 1

    @pl.when(ki == 0)
    def _():
        m_ref[...] = jnp.full(m_ref.shape, NEG_BIG, F32)
        l_ref[...] = jnp.zeros(l_ref.shape, F32)
        acc_ref[...] = jnp.zeros(acc_ref.shape, F32)

    def scores(g):
        q = q_ref[0, :, g * HEAD_DIM:(g + 1) * HEAD_DIM]
        return lax.dot_general(k_ref[0], q, (((1,), (1,)), ((), ())), preferred_element_type=F32)

    @pl.when(bounded)
    def _():
        shift = sb_ref[1].astype(F32)
        for g in range(Q_PER_KV):
            p = jnp.exp2(scores(g) - shift)
            l_ref[g] = l_ref[g] + jnp.sum(p, axis=0, keepdims=True)
            acc_ref[g] = acc_ref[g] + jnp.dot(vt_ref[0], p.astype(BF16), preferred_element_type=F32)

    @pl.when(jnp.logical_not(bounded))
    def _():
        for g in range(Q_PER_KV):
            s = scores(g)
            m_prev = m_ref[g]
            m_new = jnp.maximum(m_prev, jnp.max(s, axis=0, keepdims=True))
            alpha = jnp.exp2(m_prev - m_new)
            p = jnp.exp2(s - m_new)
            l_ref[g] = alpha * l_ref[g] + jnp.sum(p, axis=0, keepdims=True)
            acc_ref[g] = alpha * acc_ref[g] + jnp.dot(vt_ref[0], p.astype(BF16),
                                                      preferred_element_type=F32)
            m_ref[g] = m_new

    @pl.when(ki == nk - 1)
    def _():
        for g in range(Q_PER_KV):
            o = acc_ref[g] * (1.0 / l_ref[g])
            o_ref[0, :, g * HEAD_DIM:(g + 1) * HEAD_DIM] = o.T.astype(BF16)


def _score_bound(qg, kg):
    ub = (HEAD_DIM * jnp.max(jnp.abs(qg)) * jnp.max(jnp.abs(kg))
          * (LOG2E / math.sqrt(HEAD_DIM)) * (1.0 + 1e-3))
    ok = ub <= SCORE_BOUND_MAX
    shift = jnp.where(ok, jnp.ceil(ub) - SCORE_HEADROOM, 0.0)
    return jnp.stack([ok.astype(jnp.int32), shift.astype(jnp.int32)])


def _attention(q, k, vt, score_bound):
    batch, n, _ = q.shape
    tq = tk = min(ATTN_TILE, n)
    gw = Q_PER_KV * HEAD_DIM
    return pl.pallas_call(
        functools.partial(_attn_kernel, nk=n // tk),
        grid_spec=pltpu.PrefetchScalarGridSpec(
            num_scalar_prefetch=1,
            grid=(batch, N_KV_HEADS, n // tq, n // tk),
            in_specs=[
                pl.BlockSpec((1, tq, gw), lambda b, h, qi, ki, sb: (b, qi, h)),
                pl.BlockSpec((1, tk, HEAD_DIM), lambda b, h, qi, ki, sb: (b, ki, h)),
                pl.BlockSpec((1, HEAD_DIM, tk), lambda b, h, qi, ki, sb: (b, h, ki)),
            ],
            out_specs=pl.BlockSpec((1, tq, gw), lambda b, h, qi, ki, sb: (b, qi, h)),
            scratch_shapes=[
                pltpu.VMEM((Q_PER_KV, 1, tq), F32),
                pltpu.VMEM((Q_PER_KV, 1, tq), F32),
                pltpu.VMEM((Q_PER_KV, HEAD_DIM, tq), F32),
            ],
        ),
        out_shape=jax.ShapeDtypeStruct((batch, n, ATTN_WIDTH), BF16),
        compiler_params=_cparams(("arbitrary", "arbitrary", "arbitrary", "arbitrary")),
        name="attention",
    )(score_bound, q, k, vt)


def _dft1_kernel(f_ref, fc_ref, twc_ref, tws_ref, b_ref, *, n1):
    x = f_ref[0]
    a = jnp.dot(fc_ref[...].astype(BF16), x, preferred_element_type=F32)
    ar = a[:n1]
    ai = a[n1:]
    width = b_ref.shape[3]
    for t in range(b_ref.shape[1]):
        c = twc_ref[t]
        s = tws_ref[t]
        for j in range(width // LANES):
            sl = slice(t * width + j * LANES, t * width + (j + 1) * LANES)
            br = ar[:, sl] * c + ai[:, sl] * s
            bi = ai[:, sl] * c - ar[:, sl] * s
            b_ref[0, t, :, j * LANES:(j + 1) * LANES] = pltpu.pack_elementwise([br, bi], packed_dtype=BF16)


def _dft_stage3(b_ref, f3_ref, rows_ref, uv_ref):
    n2, kc, c = b_ref.shape[1:]
    n_lane_tiles = c // LANES
    for jc in range(n_lane_tiles):
        rows_ref[jc] = b_ref[0, :, :, jc * LANES:(jc + 1) * LANES].reshape(n2 * kc, LANES)
    f3 = f3_ref[...].astype(BF16)
    for j in range(kc):
        w = jnp.concatenate([rows_ref[jc, pl.ds(j, n2, stride=kc), :] for jc in range(n_lane_tiles)],
                            axis=1)
        br = pltpu.unpack_elementwise(w, index=0, packed_dtype=BF16, unpacked_dtype=F32)
        bi = pltpu.unpack_elementwise(w, index=1, packed_dtype=BF16, unpacked_dtype=F32)
        rhs = jnp.concatenate([br, bi], axis=0).astype(BF16)
        uv = jnp.dot(f3, rhs, preferred_element_type=F32)
        packed = pltpu.pack_elementwise([uv[:n2], uv[n2:]], packed_dtype=BF16)
        for jc in range(n_lane_tiles):
            uv_ref[jc, pl.ds(j, n2, stride=kc), :] = packed[:, jc * LANES:(jc + 1) * LANES]


def _dft_tables(n):
    n2 = DFT_N2
    n1 = n // n2
    scale = 1.0 / math.sqrt(n * FOURIER_GROUP_DIM)
    k1 = np.arange(n1)[:, None]
    a1 = 2.0 * np.pi * ((k1 * np.arange(n1)[None, :]) % n1) / n1
    fcat = np.concatenate([np.cos(a1), -np.sin(a1)], axis=0) * scale
    th = 2.0 * np.pi * ((np.arange(n2)[:, None] * k1.T) % n) / n
    twc = np.repeat(np.cos(th)[:, :, None], LANES, axis=2)
    tws = np.repeat(np.sin(th)[:, :, None], LANES, axis=2)
    a3 = 2.0 * np.pi * ((np.arange(n2)[:, None] * np.arange(n2)[None, :]) % n2) / n2
    c3, s3 = np.cos(a3), np.sin(a3)
    f3 = np.block([[c3, s3], [-s3, c3]])
    return jnp.asarray(fcat, F32), jnp.asarray(twc, F32), jnp.asarray(tws, F32), jnp.asarray(f3, F32)


def _dft_stage1(f2, n):
    batch = f2.shape[0]
    c = FOURIER_WIDTH
    n2 = DFT_N2
    n1 = n // n2
    g = min(n2, DFT1_ROWS // n1)
    fcat, twc, tws, f3 = _dft_tables(n)
    b = pl.pallas_call(
        functools.partial(_dft1_kernel, n1=n1),
        grid=(batch, n2 // g),
        in_specs=[
            pl.BlockSpec((1, n1, g * c), lambda b, j: (b, 0, j)),
            _const_spec((2 * n1, n1)),
            pl.BlockSpec((g, n1, LANES), lambda b, j: (j, 0, 0)),
            pl.BlockSpec((g, n1, LANES), lambda b, j: (j, 0, 0)),
        ],
        out_specs=pl.BlockSpec((1, g, n1, c), lambda b, j: (b, j, 0, 0)),
        out_shape=jax.ShapeDtypeStruct((batch, n2, n1, c), jnp.uint32),
        compiler_params=_cparams(("arbitrary", "arbitrary")),
        name="dft_stage1",
    )(f2, fcat, twc, tws)
    return b, f3


def _rms(z, g):
    ms = jnp.mean(z * z, axis=-1, keepdims=True)
    return z * lax.rsqrt(ms + EPS) * g


def _outproj_kernel(x_ref, a_ref, b_ref, f3_ref, cc_ref, sc_ref, wf_ref, ga_ref, gf_ref, wo_ref,
                    g2_ref, wr_ref, br_ref,
                    x1_ref, h2_ref, info_ref, cnt_ref, rows_ref, uv_ref):
    i = pl.program_id(0)
    n2, kc, _ = b_ref.shape[1:]
    tm = n2 * kc
    attn = a_ref[0].reshape(tm, ATTN_WIDTH).astype(F32)
    r_a = lax.rsqrt(jnp.mean(attn * attn, axis=-1, keepdims=True) + EPS)
    mix_a = jnp.dot((attn * ga_ref[...]).astype(BF16), wo_ref[0:ATTN_WIDTH, :], preferred_element_type=F32)
    _dft_stage3(b_ref, f3_ref, rows_ref, uv_ref)
    n_lane_tiles = FOURIER_WIDTH // LANES
    uvw = jnp.concatenate([uv_ref[jc] for jc in range(n_lane_tiles)], axis=1)
    u = pltpu.unpack_elementwise(uvw, index=0, packed_dtype=BF16, unpacked_dtype=F32).astype(BF16)
    v = pltpu.unpack_elementwise(uvw, index=1, packed_dtype=BF16, unpacked_dtype=F32).astype(BF16)
    cc = cc_ref[...].astype(BF16)
    sc = sc_ref[...].astype(BF16)
    parts = []
    for g in range(N_FOURIER_GROUPS):
        sl = slice(g * FOURIER_GROUP_DIM, (g + 1) * FOURIER_GROUP_DIM)
        parts.append(jnp.dot(u[:, sl], cc, preferred_element_type=F32)
                     + jnp.dot(v[:, sl], sc, preferred_element_type=F32))
    fmix = jnp.concatenate(parts, axis=-1).astype(BF16)
    four = jnp.dot(fmix, wf_ref[...], preferred_element_type=F32)
    r_f = lax.rsqrt(jnp.mean(four * four, axis=-1, keepdims=True) + EPS)
    mix_f = jnp.dot((four * gf_ref[...]).astype(BF16), wo_ref[ATTN_WIDTH:D_MODEL, :],
                    preferred_element_type=F32)
    x1 = x_ref[0].reshape(tm, D_MODEL) + (r_a * mix_a + r_f * mix_f)
    x1_ref[...] = x1
    x1g = x1 * g2_ref[...]
    r_2 = lax.rsqrt(jnp.mean(x1 * x1, axis=-1, keepdims=True) + EPS)
    h2f = x1g * r_2
    h2_ref[...] = pltpu.pack_elementwise([h2f[:, :D_MODEL // 2], h2f[:, D_MODEL // 2:]], packed_dtype=BF16)

    lg = r_2 * jnp.dot(x1g.astype(BF16), wr_ref[...], preferred_element_type=F32) + br_ref[...]
    lane = lax.broadcasted_iota(jnp.int32, lg.shape, 1).astype(F32)
    is_g = lane < N_EXPERT_GROUPS
    gl = jnp.where(is_g, lg, NEG_BIG)
    gmax = jnp.max(gl, axis=-1, keepdims=True)
    gidx = jnp.min(jnp.where(gl == gmax, lane, float(LANES)), axis=-1, keepdims=True)
    gsum = jnp.sum(jnp.where(is_g, jnp.exp(gl - gmax), 0.0), axis=-1, keepdims=True)
    g_w = 1.0 / gsum
    lo_lane = N_EXPERT_GROUPS + EXPERTS_PER_GROUP * gidx
    in_grp = jnp.logical_and(lane >= lo_lane, lane < lo_lane + EXPERTS_PER_GROUP)
    el = jnp.where(in_grp, lg, NEG_BIG)
    m1 = jnp.max(el, axis=-1, keepdims=True)
    i1 = jnp.min(jnp.where(el == m1, lane, float(LANES)), axis=-1, keepdims=True)
    el2 = jnp.where(lane == i1, NEG_BIG, el)
    m2 = jnp.max(el2, axis=-1, keepdims=True)
    i2 = jnp.min(jnp.where(el2 == m2, lane, float(LANES)), axis=-1, keepdims=True)
    r = jnp.exp(m2 - m1)
    w1 = g_w / (1.0 + r)
    w2 = g_w * r / (1.0 + r)
    first_lo = i1 < i2
    ea = jnp.minimum(i1, i2) - lo_lane
    eb = jnp.maximum(i1, i2) - lo_lane
    pair = ea * (7.0 - ea) * 0.5 + (eb - ea - 1.0)
    bucket = gidx * float(N_PAIRS) + pair
    w_lo = jnp.where(first_lo, w1, w2)
    w_hi = jnp.where(first_lo, w2, w1)
    info_ref[...] = jnp.where(lane == 0.0, bucket,
                              jnp.where(lane == 1.0, w_lo, jnp.where(lane == 2.0, w_hi, 0.0)))
    cnt = jnp.sum(jnp.where(lane == bucket, 1.0, 0.0), axis=0, keepdims=True)

    @pl.when(i == 0)
    def _():
        cnt_ref[...] = jnp.zeros(cnt_ref.shape, F32)

    cnt_ref[...] = cnt_ref[...] + cnt


def _out_proj(x, attn, b, f3, cc, sc, w_fmix, ga, gf, w_out, g2, w_router, b_router):
    batch, n, _ = x.shape
    n2 = DFT_N2
    n1 = n // n2
    kc = DFT3_ROWS
    nj = n1 // kc
    tm = n2 * kc
    assert tm == TOKEN_TILE
    t = batch * n
    row = lambda i: (i, 0)
    grp = lambda i: (i // nj, 0, i % nj, 0)
    n_lane_tiles = FOURIER_WIDTH // LANES
    return pl.pallas_call(
        _outproj_kernel,
        grid=(t // tm,),
        in_specs=[
            pl.BlockSpec((1, n2, kc, D_MODEL), grp),
            pl.BlockSpec((1, n2, kc, ATTN_WIDTH), grp),
            pl.BlockSpec((1, n2, kc, FOURIER_WIDTH), grp),
            _const_spec((2 * n2, 2 * n2)),
            _const_spec((FOURIER_GROUP_DIM, FOURIER_GROUP_DIM)),
            _const_spec((FOURIER_GROUP_DIM, FOURIER_GROUP_DIM)),
            _const_spec((FOURIER_WIDTH, FOURIER_WIDTH)),
            _const_spec((1, ATTN_WIDTH)),
            _const_spec((1, FOURIER_WIDTH)),
            _const_spec((D_MODEL, D_MODEL)),
            _const_spec((1, D_MODEL)),
            _const_spec((D_MODEL, LANES)),
            _const_spec((1, LANES)),
        ],
        out_specs=[
            pl.BlockSpec((tm, D_MODEL), row),
            pl.BlockSpec((tm, D_MODEL // 2), row),
            pl.BlockSpec((tm, LANES), row),
            pl.BlockSpec((8, LANES), lambda i: (0, 0)),
        ],
        out_shape=[
            jax.ShapeDtypeStruct((t, D_MODEL), F32),
            jax.ShapeDtypeStruct((t, D_MODEL // 2), jnp.uint32),
            jax.ShapeDtypeStruct((t, LANES), F32),
            jax.ShapeDtypeStruct((8, LANES), F32),
        ],
        scratch_shapes=[
            pltpu.VMEM((n_lane_tiles, tm, LANES), jnp.uint32),
            pltpu.VMEM((n_lane_tiles, tm, LANES), jnp.uint32),
        ],
        compiler_params=_cparams(("arbitrary",)),
        name="out_proj_router",
    )(x.reshape(batch, n2, n1, D_MODEL), attn.reshape(batch, n2, n1, ATTN_WIDTH), b, f3,
      cc, sc, w_fmix, ga, gf, w_out, g2, w_router, b_router)


def _rank_kernel(info_ref, start_ref, pos_ref, base_ref):
    i = pl.program_id(0)
    tm = info_ref.shape[0]

    @pl.when(i == 0)
    def _():
        base_ref[...] = start_ref[...]

    bucket = info_ref[...].T[0:1, :]
    brow = lax.broadcasted_iota(jnp.int32, (32, tm), 0).astype(F32)
    onehot = jnp.where(brow == bucket, 1.0, 0.0)
    oh = onehot.astype(BF16)
    src = lax.broadcasted_iota(jnp.int32, (tm, tm), 0)
    dst = lax.broadcasted_iota(jnp.int32, (tm, tm), 1)
    upper = jnp.where(src <= dst, 1.0, 0.0).astype(BF16)
    cum = jnp.dot(oh, upper, preferred_element_type=F32)
    base = base_ref[...]
    base_t = jnp.concatenate([base] * (tm // LANES), axis=1)
    pos = jnp.sum(onehot * (cum - 1.0 + base_t), axis=0, keepdims=True)
    pos_ref[0] = pos.astype(jnp.int32)
    ones = jnp.ones((tm, LANES), BF16)
    base_ref[...] = base + jnp.dot(oh, ones, preferred_element_type=F32)


def _rank(info, starts):
    t = info.shape[0]
    tm = TOKEN_TILE
    return pl.pallas_call(
        _rank_kernel,
        grid=(t // tm,),
        in_specs=[
            pl.BlockSpec((tm, LANES), lambda i: (i, 0)),
            pl.BlockSpec((32, LANES), lambda i: (0, 0)),
        ],
        out_specs=pl.BlockSpec((1, 1, tm), lambda i: (i, 0, 0)),
        out_shape=jax.ShapeDtypeStruct((t // tm, 1, tm), jnp.int32),
        scratch_shapes=[pltpu.VMEM((32, LANES), F32)],
        compiler_params=_cparams(("arbitrary",)),
        name="rank",
    )(info, starts)


def _for_each_row(n_rows, fn):
    def body(c, carry):
        for u in range(ROW_DMA_UNROLL):
            fn(c * ROW_DMA_UNROLL + u)
        return carry

    lax.fori_loop(0, n_rows // ROW_DMA_UNROLL, body, 0)


def _scatter_kernel(pad_ref, pos_ref, ha_ref, hb_ref, xs_ref, zero_ref, sem, zsem, *, steps_a, n_steps,
                    n_tiles):
    i = pl.program_id(0)
    tm = ha_ref.shape[0]
    r_tile = zero_ref.shape[0]

    def scatter_from(h_ref):
        def row_copy(r):
            return pltpu.make_async_copy(h_ref.at[pl.ds(r, 1)], xs_ref.at[pl.ds(pos_ref[0, 0, r], 1)], sem)

        _for_each_row(tm, lambda r: row_copy(r).start())
        _for_each_row(tm, lambda r: row_copy(r).wait())

    @pl.when(i < steps_a)
    def _():
        scatter_from(ha_ref)

    @pl.when(i >= steps_a)
    def _():
        scatter_from(hb_ref)

    @pl.when(i == n_steps - 1)
    def _():
        zero = jnp.zeros(zero_ref.shape, F32)
        zero_ref[...] = pltpu.pack_elementwise([zero, zero], packed_dtype=BF16)

        def pad_copy(r):
            return pltpu.make_async_copy(zero_ref.at[pl.ds(0, 1)], xs_ref.at[pl.ds(r, 1)], zsem)

        def tile_copy(t):
            return pltpu.make_async_copy(zero_ref, xs_ref.at[pl.ds(pl.multiple_of(t * r_tile, r_tile), r_tile)],
                                         zsem)

        def start_row(r, carry):
            pad_copy(r).start()
            return carry

        def wait_row(r, carry):
            pad_copy(r).wait()
            return carry

        def start_tile(t, carry):
            tile_copy(t).start()
            return carry

        def wait_tile(t, carry):
            tile_copy(t).wait()
            return carry

        for b in range(N_BUCKETS):
            lax.fori_loop(pad_ref[2 * b], pad_ref[2 * b + 1], start_row, 0)
        lax.fori_loop(pad_ref[2 * N_BUCKETS], n_tiles, start_tile, 0)
        for b in range(N_BUCKETS):
            lax.fori_loop(pad_ref[2 * b], pad_ref[2 * b + 1], wait_row, 0)
        lax.fori_loop(pad_ref[2 * N_BUCKETS], n_tiles, wait_tile, 0)


def _scatter(pad, pos, h_a, h_b, n_tiles):
    w = h_a.shape[1]
    tm = TOKEN_TILE
    steps_a = h_a.shape[0] // tm
    n_steps = steps_a + h_b.shape[0] // tm
    r = EXPERT_TILE
    return pl.pallas_call(
        functools.partial(_scatter_kernel, steps_a=steps_a, n_steps=n_steps, n_tiles=n_tiles),
        grid_spec=pltpu.PrefetchScalarGridSpec(
            num_scalar_prefetch=1,
            grid=(n_steps,),
            in_specs=[
                pl.BlockSpec((1, 1, tm), lambda i, pad: (i, 0, 0), memory_space=pltpu.SMEM),
                pl.BlockSpec((tm, w), lambda i, pad: (jnp.minimum(i, steps_a - 1), 0)),
                pl.BlockSpec((tm, w), lambda i, pad: (jnp.maximum(i - steps_a, 0), 0)),
            ],
            out_specs=pl.BlockSpec(memory_space=pl.ANY),
            scratch_shapes=[pltpu.VMEM((r, w), jnp.uint32), pltpu.SemaphoreType.DMA,
                            pltpu.SemaphoreType.DMA],
        ),
        out_shape=jax.ShapeDtypeStruct((n_tiles * r, w), jnp.uint32),
        compiler_params=_cparams(("arbitrary",)),
        name="scatter_rows",
    )(pad, pos, h_a, h_b)


def _expert_kernel(lo_ref, hi_ref, blk_ref, valid_ref, x_ref,
                   wg0_ref, wu0_ref, wd0_ref, wg1_ref, wu1_ref, wd1_ref, o_ref):
    del lo_ref, hi_ref, blk_ref
    i = pl.program_id(0)

    @pl.when(valid_ref[i] == 1)
    def _():
        xw = x_ref[...]
        x = jnp.concatenate(
            [pltpu.unpack_elementwise(xw, index=0, packed_dtype=BF16, unpacked_dtype=F32),
             pltpu.unpack_elementwise(xw, index=1, packed_dtype=BF16, unpacked_dtype=F32)],
            axis=1).astype(BF16)

        def mlp(wg_ref, wu_ref, wd_ref):
            a = jnp.dot(x, wg_ref[0], preferred_element_type=F32)
            u = jnp.dot(x, wu_ref[0], preferred_element_type=F32)
            h = (a * (1.0 / (1.0 + jnp.exp(-a))) * u).astype(BF16)
            return jnp.dot(h, wd_ref[0], preferred_element_type=F32)

        e_lo = mlp(wg0_ref, wu0_ref, wd0_ref)
        e_hi = mlp(wg1_ref, wu1_ref, wd1_ref)
        o_ref[...] = pltpu.pack_elementwise([e_lo, e_hi], packed_dtype=BF16)

    @pl.when(valid_ref[i] == 0)
    def _():
        zero = jnp.zeros(o_ref.shape, F32)
        o_ref[...] = pltpu.pack_elementwise([zero, zero], packed_dtype=BF16)


def _experts(lo, hi, blk, valid, xs, wg, wu, wd):
    n_tiles = lo.shape[0]
    r = EXPERT_TILE
    x_map = lambda i, lo, hi, blk, valid: (blk[i], 0)
    lo_map = lambda i, lo, hi, blk, valid: (lo[i], 0, 0)
    hi_map = lambda i, lo, hi, blk, valid: (hi[i], 0, 0)
    one = pl.Buffered(1)
    w_up = (1, D_MODEL, EXPERT_FF)
    w_dn = (1, EXPERT_FF, D_MODEL)
    return pl.pallas_call(
        _expert_kernel,
        grid_spec=pltpu.PrefetchScalarGridSpec(
            num_scalar_prefetch=4,
            grid=(n_tiles,),
            in_specs=[
                pl.BlockSpec((r, D_MODEL // 2), x_map),
                pl.BlockSpec(w_up, lo_map, pipeline_mode=one),
                pl.BlockSpec(w_up, lo_map, pipeline_mode=one),
                pl.BlockSpec(w_dn, lo_map, pipeline_mode=one),
                pl.BlockSpec(w_up, hi_map),
                pl.BlockSpec(w_up, hi_map),
                pl.BlockSpec(w_dn, hi_map),
            ],
            out_specs=pl.BlockSpec((r, D_MODEL), lambda i, lo, hi, blk, valid: (i, 0)),
        ),
        out_shape=jax.ShapeDtypeStruct((n_tiles * r, D_MODEL), jnp.uint32),
        compiler_params=_cparams(("arbitrary",)),
        name="experts",
    )(lo, hi, blk, valid, xs, wg, wu, wd, wg, wu, wd)


def _combine_kernel(pos_ref, nxt_ref, x1_ref, info_ref, eo_ref, y_ref, gbuf, sems, *, n_steps):
    i = pl.program_id(0)
    tm = x1_ref.shape[0]
    slot = lax.rem(i, 2)

    def row_copy(p_ref, s, r):
        return pltpu.make_async_copy(eo_ref.at[pl.ds(p_ref[0, 0, r], 1)], gbuf.at[s, pl.ds(r, 1)],
                                     sems.at[s])

    @pl.when(i == 0)
    def _():
        _for_each_row(tm, lambda r: row_copy(pos_ref, 0, r).start())

    @pl.when(i + 1 < n_steps)
    def _():
        _for_each_row(tm, lambda r: row_copy(nxt_ref, 1 - slot, r).start())

    _for_each_row(tm, lambda r: row_copy(pos_ref, slot, r).wait())

    g = gbuf[slot]
    e_lo = pltpu.unpack_elementwise(g, index=0, packed_dtype=BF16, unpacked_dtype=F32)
    e_hi = pltpu.unpack_elementwise(g, index=1, packed_dtype=BF16, unpacked_dtype=F32)
    info = info_ref[...]
    y = x1_ref[...] + info[:, 1:2] * e_lo + info[:, 2:3] * e_hi
    y_ref[0] = y.reshape(y_ref.shape[1:])


def _combine(pos, x1, info, eo, batch, n):
    n2 = DFT_N2
    n1 = n // n2
    kc = DFT3_ROWS
    nj = n1 // kc
    tm = n2 * kc
    n_steps = batch * nj
    return pl.pallas_call(
        functools.partial(_combine_kernel, n_steps=n_steps),
        grid=(n_steps,),
        in_specs=[
            pl.BlockSpec((1, 1, tm), lambda i: (i, 0, 0), memory_space=pltpu.SMEM),
            pl.BlockSpec((1, 1, tm), lambda i: (jnp.minimum(i + 1, n_steps - 1), 0, 0),
                         memory_space=pltpu.SMEM),
            pl.BlockSpec((tm, D_MODEL), lambda i: (i, 0)),
            pl.BlockSpec((tm, LANES), lambda i: (i, 0)),
            pl.BlockSpec(memory_space=pl.ANY),
        ],
        out_specs=pl.BlockSpec((1, n2, kc, D_MODEL), lambda i: (i // nj, 0, i % nj, 0)),
        out_shape=jax.ShapeDtypeStruct((batch, n2, n1, D_MODEL), F32),
        scratch_shapes=[pltpu.VMEM((2, tm, D_MODEL), jnp.uint32), pltpu.SemaphoreType.DMA((2,))],
        compiler_params=_cparams(("arbitrary",)),
        name="combine",
    )(pos, pos, x1, info, eo)


def _rope_tables(n):
    rows = n // GRID_W
    row_id = np.repeat(np.arange(rows, dtype=np.float64), GRID_W)
    col_id = np.tile(np.arange(GRID_W, dtype=np.float64), rows)
    inv_freq = np.exp(-math.log(ROPE_THETA) * 2.0 * np.arange(ROPE_FREQS, dtype=np.float64) / (2 * ROPE_FREQS))
    ar = row_id[:, None] * inv_freq
    ac = col_id[:, None] * inv_freq
    cr, sr, ccol, scol = np.cos(ar), np.sin(ar), np.cos(ac), np.sin(ac)
    zero = np.zeros_like(sr)
    c = np.concatenate([cr, cr, ccol, ccol], axis=1)
    s1 = np.concatenate([-sr, zero, -scol, zero], axis=1)
    s2 = np.concatenate([zero, sr, zero, scol], axis=1)
    return jnp.asarray(c, F32), jnp.asarray(s1, F32), jnp.asarray(s2, F32)


def _channel_dft_tables():
    g = FOURIER_GROUP_DIM
    a = 2.0 * np.pi * ((np.arange(g)[:, None] * np.arange(g)[None, :]) % g) / g
    return jnp.asarray(np.cos(a), F32), jnp.asarray(np.sin(a), F32)


_PAIR_LO = np.array([0, 0, 0, 1, 1, 2], np.int32)
_PAIR_HI = np.array([1, 2, 3, 2, 3, 3], np.int32)


def kernel(x_prompt, x_sample, norm1_g, w_in, q_norm_g, k_norm_g, w_fmix, attn_out_g, fourier_out_g,
           w_out, norm2_g, w_router_group, b_router_group, w_router_expert, b_router_expert,
           w_gate, w_up, w_down):
    l = 0
    g1 = norm1_g[l][None, :]
    g2 = norm2_g[l][None, :]
    qg = q_norm_g[l][None, :]
    kg = k_norm_g[l][None, :]
    ga = attn_out_g[l][None, :]
    gf = fourier_out_g[l][None, :]
    w_in_b = w_in[l].astype(BF16)
    w_fmix_b = w_fmix[l].astype(BF16)
    w_out_b = w_out[l].astype(BF16)
    wg_b = w_gate[l].astype(BF16)
    wu_b = w_up[l].astype(BF16)
    wd_b = w_down[l].astype(BF16)
    n_r = N_EXPERT_GROUPS + N_EXPERTS
    w_router = jnp.zeros((D_MODEL, LANES), F32)
    w_router = w_router.at[:, :N_EXPERT_GROUPS].set(w_router_group[l])
    w_router = w_router.at[:, N_EXPERT_GROUPS:n_r].set(w_router_expert[l]).astype(BF16)
    b_router = jnp.zeros((1, LANES), F32)
    b_router = b_router.at[0, :N_EXPERT_GROUPS].set(b_router_group[l])
    b_router = b_router.at[0, N_EXPERT_GROUPS:n_r].set(b_router_expert[l])
    cc, sc = _channel_dft_tables()
    score_bound = _score_bound(qg, kg)

    seqs = []
    for x in (x_prompt, x_sample):
        batch, n, _ = x.shape
        x2d = x.reshape(batch * n, D_MODEL)
        rope_c, rope_s1, rope_s2 = _rope_tables(n)
        q, k, vt, f2 = _in_proj(x2d, n, g1, w_in_b, qg, kg, rope_c, rope_s1, rope_s2)
        attn = _attention(q.reshape(batch, n, ATTN_WIDTH), k.reshape(batch, n, KV_WIDTH), vt, score_bound)
        b, f3 = _dft_stage1(f2, n)
        x1, h2, info, cnt = _out_proj(x, attn, b, f3, cc, sc, w_fmix_b, ga, gf, w_out_b, g2, w_router, b_router)
        seqs.append((x.shape, x1, h2, info, cnt))

    r = EXPERT_TILE
    t_all = sum(s[1].shape[0] for s in seqs)
    n_tiles = t_all // r + N_BUCKETS
    counts = sum(s[4][0, :N_BUCKETS] for s in seqs).astype(jnp.int32)
    tiles_b = (counts + (r - 1)) // r
    tile_end = jnp.cumsum(tiles_b)
    tile_start = tile_end - tiles_b
    total_tiles = tile_end[-1]
    starts = jnp.zeros((32,), F32).at[:N_BUCKETS].set((tile_start * r).astype(F32))
    starts = jnp.broadcast_to(starts[:, None], (32, LANES))
    tile_id = jnp.arange(n_tiles, dtype=jnp.int32)
    blk = jnp.minimum(tile_id, total_tiles - 1)
    tile_bucket = jnp.sum((tile_end[None, :] <= blk[:, None]).astype(jnp.int32), axis=1)
    grp = tile_bucket // N_PAIRS
    pair = tile_bucket % N_PAIRS
    lo = grp * EXPERTS_PER_GROUP + jnp.asarray(_PAIR_LO)[pair]
    hi = grp * EXPERTS_PER_GROUP + jnp.asarray(_PAIR_HI)[pair]
    valid = (tile_id < total_tiles).astype(jnp.int32)

    info_all = jnp.concatenate([s[3] for s in seqs], axis=0)
    pos = _rank(info_all, starts)

    pad = jnp.concatenate([jnp.stack([tile_start * r + counts, tile_end * r], axis=1).reshape(-1),
                           total_tiles[None]]).astype(jnp.int32)
    xs = _scatter(pad, pos, seqs[0][2], seqs[1][2], n_tiles)
    eo = _experts(lo, hi, blk, valid, xs, wg_b, wu_b, wd_b)

    outs = []
    off = 0
    for shape, x1, _, info, _ in seqs:
        nt = x1.shape[0] // TOKEN_TILE
        y = _combine(pos[off:off + nt], x1, info, eo, shape[0], shape[1])
        outs.append(y.reshape(shape))
        off += nt
    return tuple(outs)
```

```python
import functools
import math

import numpy as np
import jax
import jax.numpy as jnp
from jax import lax
from jax.experimental import pallas as pl
from jax.experimental.pallas import tpu as pltpu

F32 = jnp.float32
BF16 = jnp.bfloat16

EPS = 1e-6
D_MODEL = 2048
HEAD_DIM = 128
N_HEADS = 8
N_KV_HEADS = 2
Q_PER_KV = N_HEADS // N_KV_HEADS
ATTN_WIDTH = N_HEADS * HEAD_DIM
KV_WIDTH = N_KV_HEADS * HEAD_DIM
FOURIER_WIDTH = D_MODEL - ATTN_WIDTH
N_FOURIER_GROUPS = 4
FOURIER_GROUP_DIM = FOURIER_WIDTH // N_FOURIER_GROUPS
IN_WIDTH = ATTN_WIDTH + 2 * KV_WIDTH + FOURIER_WIDTH
GRID_W = 64
ROPE_THETA = 10000.0
ROPE_FREQS = HEAD_DIM // 4
N_EXPERT_GROUPS = 4
EXPERTS_PER_GROUP = 4
N_EXPERTS = N_EXPERT_GROUPS * EXPERTS_PER_GROUP
EXPERT_FF = D_MODEL // 2
N_PAIRS = 6
N_BUCKETS = N_EXPERT_GROUPS * N_PAIRS

LANES = 128
TOKEN_TILE = 512
ATTN_TILE = 1024
EXPERT_TILE = 256
DFT_N2 = 32
DFT1_ROWS = 512
DFT3_ROWS = 16
ROW_DMA_UNROLL = 8
VMEM_LIMIT = 56 * 1024 * 1024
NEG_BIG = -1e30
LOG2E = 1.4426950408889634
SCORE_HEADROOM = 64
SCORE_BOUND_MAX = 87.0


def _cparams(sem, vmem=VMEM_LIMIT):
    return pltpu.CompilerParams(dimension_semantics=sem, vmem_limit_bytes=vmem)


def _const_spec(shape):
    nd = len(shape)
    return pl.BlockSpec(shape, lambda *_: (0,) * nd, pipeline_mode=pl.Buffered(1))


def _inproj_kernel(x_ref, g1_ref, w_ref, qg_ref, kg_ref, c_ref, s1_ref, s2_ref, perm_ref,
                   q_ref, k_ref, vt_ref, f_ref, *, q_scale):
    x = x_ref[...]
    r = lax.rsqrt(jnp.mean(x * x, axis=-1, keepdims=True) + EPS)
    h = (x * g1_ref[...]).astype(BF16)
    c = c_ref[...]
    s1 = s1_ref[...]
    s2 = s2_ref[...]

    def norm_rope(z, g):
        zms = jnp.mean(z * z, axis=-1, keepdims=True)
        zn = z * lax.rsqrt(zms + EPS) * g
        return zn * c + pltpu.roll(zn, 96, 1) * s1 + pltpu.roll(zn, 32, 1) * s2

    qg = qg_ref[...]
    kg = kg_ref[...]
    pq = r * jnp.dot(h, w_ref[:, 0:ATTN_WIDTH], preferred_element_type=F32)
    for hh in range(N_HEADS):
        sl = slice(hh * HEAD_DIM, (hh + 1) * HEAD_DIM)
        q_ref[:, sl] = (norm_rope(pq[:, sl], qg) * q_scale).astype(BF16)
    pk = r * jnp.dot(h, w_ref[:, ATTN_WIDTH:ATTN_WIDTH + KV_WIDTH], preferred_element_type=F32)
    for hh in range(N_KV_HEADS):
        sl = slice(hh * HEAD_DIM, (hh + 1) * HEAD_DIM)
        k_ref[:, sl] = norm_rope(pk[:, sl], kg).astype(BF16)
    pv = r * jnp.dot(h, w_ref[:, ATTN_WIDTH + KV_WIDTH:ATTN_WIDTH + 2 * KV_WIDTH],
                     preferred_element_type=F32)
    vt_ref[0] = pv.T.astype(BF16)
    pf = r * jnp.dot(h, w_ref[:, ATTN_WIDTH + 2 * KV_WIDTH:IN_WIDTH], preferred_element_type=F32)
    pfp = jnp.dot(perm_ref[...], pf.astype(BF16), preferred_element_type=F32).astype(BF16)
    rows = pfp.shape[0] // DFT_N2
    for t2 in range(DFT_N2):
        f_ref[0, :, t2 * FOURIER_WIDTH:(t2 + 1) * FOURIER_WIDTH] = pfp[t2 * rows:(t2 + 1) * rows, :]


def _in_proj(x2d, n, g1, w_in, qg, kg, rope_c, rope_s1, rope_s2):
    t = x2d.shape[0]
    tm = TOKEN_TILE
    nt = n // tm
    batch = t // n
    q_scale = LOG2E / math.sqrt(HEAD_DIM)
    r_out = np.arange(tm)
    perm = np.zeros((tm, tm), np.float32)
    perm[r_out, (r_out % (tm // DFT_N2)) * DFT_N2 + r_out // (tm // DFT_N2)] = 1.0
    perm = jnp.asarray(perm, BF16)
    return pl.pallas_call(
        functools.partial(_inproj_kernel, q_scale=q_scale),
        grid=(t // tm,),
        in_specs=[
            pl.BlockSpec((tm, D_MODEL), lambda i: (i, 0)),
            _const_spec((1, D_MODEL)),
            _const_spec((D_MODEL, IN_WIDTH)),
            _const_spec((1, HEAD_DIM)),
            _const_spec((1, HEAD_DIM)),
            pl.BlockSpec((tm, HEAD_DIM), lambda i: (i % nt, 0)),
            pl.BlockSpec((tm, HEAD_DIM), lambda i: (i % nt, 0)),
            pl.BlockSpec((tm, HEAD_DIM), lambda i: (i % nt, 0)),
            _const_spec((tm, tm)),
        ],
        out_specs=[
            pl.BlockSpec((tm, ATTN_WIDTH), lambda i: (i, 0)),
            pl.BlockSpec((tm, KV_WIDTH), lambda i: (i, 0)),
            pl.BlockSpec((1, KV_WIDTH, tm), lambda i: (i // nt, 0, i % nt)),
            pl.BlockSpec((1, tm // DFT_N2, DFT_N2 * FOURIER_WIDTH), lambda i: (i // nt, i % nt, 0)),
        ],
        out_shape=[
            jax.ShapeDtypeStruct((t, ATTN_WIDTH), BF16),
            jax.ShapeDtypeStruct((t, KV_WIDTH), BF16),
            jax.ShapeDtypeStruct((batch, KV_WIDTH, n), BF16),
            jax.ShapeDtypeStruct((batch, n // DFT_N2, DFT_N2 * FOURIER_WIDTH), BF16),
        ],
        compiler_params=_cparams(("arbitrary",)),
        name="in_proj",
    )(x2d, g1, w_in, qg, kg, rope_c, rope_s1, rope_s2, perm)


def _attn_kernel(sb_ref, q_ref, k_ref, vt_ref, o_ref, m_ref, l_ref, acc_ref, *, nk):
    ki = pl.program_id(3)
    bounded = sb_ref[0] == 1

    @pl.when(ki == 0)
    def _():
        m_ref[...] = jnp.full(m_ref.shape, NEG_BIG, F32)
        l_ref[...] = jnp.zeros(l_ref.shape, F32)
        acc_ref[...] = jnp.zeros(acc_ref.shape, F32)

    def scores(g):
        q = q_ref[0, :, g * HEAD_DIM:(g + 1) * HEAD_DIM]
        return lax.dot_general(k_ref[0], q, (((1,), (1,)), ((), ())), preferred_element_type=F32)

    @pl.when(bounded)
    def _():
        shift = sb_ref[1].astype(F32)
        for g in range(Q_PER_KV):
            p = jnp.exp2(scores(g) - shift)
            l_ref[g] = l_ref[g] + jnp.sum(p, axis=0, keepdims=True)
            acc_ref[g] = acc_ref[g] + jnp.dot(vt_ref[0], p.astype(BF16), preferred_element_type=F32)

    @pl.when(jnp.logical_not(bounded))
    def _():
        for g in range(Q_PER_KV):
            s = scores(g)
            m_prev = m_ref[g]
            m_new = jnp.maximum(m_prev, jnp.max(s, axis=0, keepdims=True))
            alpha = jnp.exp2(m_prev - m_new)
            p = jnp.exp2(s - m_new)
            l_ref[g] = alpha * l_ref[g] + jnp.sum(p, axis=0, keepdims=True)
            acc_ref[g] = alpha * acc_ref[g] + jnp.dot(vt_ref[0], p.astype(BF16),
                                                      preferred_element_type=F32)
            m_ref[g] = m_new

    @pl.when(ki == nk - 1)
    def _():
        for g in range(Q_PER_KV):
            o = acc_ref[g] * (1.0 / l_ref[g])
            o_ref[0, :, g * HEAD_DIM:(g + 1) * HEAD_DIM] = o.T.astype(BF16)


def _score_bound(qg, kg):
    ub = (HEAD_DIM * jnp.max(jnp.abs(qg)) * jnp.max(jnp.abs(kg))
          * (LOG2E / math.sqrt(HEAD_DIM)) * (1.0 + 1e-3))
    ok = ub <= SCORE_BOUND_MAX
    shift = jnp.where(ok, jnp.ceil(ub) - SCORE_HEADROOM, 0.0)
    return jnp.stack([ok.astype(jnp.int32), shift.astype(jnp.int32)])


def _attention(q, k, vt, score_bound):
    batch, n, _ = q.shape
    tq = tk = min(ATTN_TILE, n)
    gw = Q_PER_KV * HEAD_DIM
    return pl.pallas_call(
        functools.partial(_attn_kernel, nk=n // tk),
        grid_spec=pltpu.PrefetchScalarGridSpec(
            num_scalar_prefetch=1,
            grid=(batch, N_KV_HEADS, n // tq, n // tk),
            in_specs=[
                pl.BlockSpec((1, tq, gw), lambda b, h, qi, ki, sb: (b, qi, h)),
                pl.BlockSpec((1, tk, HEAD_DIM), lambda b, h, qi, ki, sb: (b, ki, h)),
                pl.BlockSpec((1, HEAD_DIM, tk), lambda b, h, qi, ki, sb: (b, h, ki)),
            ],
            out_specs=pl.BlockSpec((1, tq, gw), lambda b, h, qi, ki, sb: (b, qi, h)),
            scratch_shapes=[
                pltpu.VMEM((Q_PER_KV, 1, tq), F32),
                pltpu.VMEM((Q_PER_KV, 1, tq), F32),
                pltpu.VMEM((Q_PER_KV, HEAD_DIM, tq), F32),
            ],
        ),
        out_shape=jax.ShapeDtypeStruct((batch, n, ATTN_WIDTH), BF16),
        compiler_params=_cparams(("arbitrary", "arbitrary", "arbitrary", "arbitrary")),
        name="attention",
    )(score_bound, q, k, vt)


def _dft1_kernel(f_ref, fc_ref, twc_ref, tws_ref, b_ref, *, n1):
    x = f_ref[0]
    a = jnp.dot(fc_ref[...].astype(BF16), x, preferred_element_type=F32)
    ar = a[:n1]
    ai = a[n1:]
    width = b_ref.shape[3]
    for t in range(b_ref.shape[1]):
        c = twc_ref[t]
        s = tws_ref[t]
        for j in range(width // LANES):
            sl = slice(t * width + j * LANES, t * width + (j + 1) * LANES)
            br = ar[:, sl] * c + ai[:, sl] * s
            bi = ai[:, sl] * c - ar[:, sl] * s
            b_ref[0, t, :, j * LANES:(j + 1) * LANES] = pltpu.pack_elementwise([br, bi], packed_dtype=BF16)


def _dft_stage3(b_ref, f3_ref, rows_ref, uv_ref):
    n2, kc, c = b_ref.shape[1:]
    n_lane_tiles = c // LANES
    for jc in range(n_lane_tiles):
        rows_ref[jc] = b_ref[0, :, :, jc * LANES:(jc + 1) * LANES].reshape(n2 * kc, LANES)
    f3 = f3_ref[...].astype(BF16)
    for j in range(kc):
        w = jnp.concatenate([rows_ref[jc, pl.ds(j, n2, stride=kc), :] for jc in range(n_lane_tiles)],
                            axis=1)
        br = pltpu.unpack_elementwise(w, index=0, packed_dtype=BF16, unpacked_dtype=F32)
        bi = pltpu.unpack_elementwise(w, index=1, packed_dtype=BF16, unpacked_dtype=F32)
        rhs = jnp.concatenate([br, bi], axis=0).astype(BF16)
        uv = jnp.dot(f3, rhs, preferred_element_type=F32)
        packed = pltpu.pack_elementwise([uv[:n2], uv[n2:]], packed_dtype=BF16)
        for jc in range(n_lane_tiles):
            uv_ref[jc, pl.ds(j, n2, stride=kc), :] = packed[:, jc * LANES:(jc + 1) * LANES]


def _dft_tables(n):
    n2 = DFT_N2
    n1 = n // n2
    scale = 1.0 / math.sqrt(n * FOURIER_GROUP_DIM)
    k1 = np.arange(n1)[:, None]
    a1 = 2.0 * np.pi * ((k1 * np.arange(n1)[None, :]) % n1) / n1
    fcat = np.concatenate([np.cos(a1), -np.sin(a1)], axis=0) * scale
    th = 2.0 * np.pi * ((np.arange(n2)[:, None] * k1.T) % n) / n
    twc = np.repeat(np.cos(th)[:, :, None], LANES, axis=2)
    tws = np.repeat(np.sin(th)[:, :, None], LANES, axis=2)
    a3 = 2.0 * np.pi * ((np.arange(n2)[:, None] * np.arange(n2)[None, :]) % n2) / n2
    c3, s3 = np.cos(a3), np.sin(a3)
    f3 = np.block([[c3, s3], [-s3, c3]])
    return jnp.asarray(fcat, F32), jnp.asarray(twc, F32), jnp.asarray(tws, F32), jnp.asarray(f3, F32)


def _dft_stage1(f2, n):
    batch = f2.shape[0]
    c = FOURIER_WIDTH
    n2 = DFT_N2
    n1 = n // n2
    g = min(n2, DFT1_ROWS // n1)
    fcat, twc, tws, f3 = _dft_tables(n)
    b = pl.pallas_call(
        functools.partial(_dft1_kernel, n1=n1),
        grid=(batch, n2 // g),
        in_specs=[
            pl.BlockSpec((1, n1, g * c), lambda b, j: (b, 0, j)),
            _const_spec((2 * n1, n1)),
            pl.BlockSpec((g, n1, LANES), lambda b, j: (j, 0, 0)),
            pl.BlockSpec((g, n1, LANES), lambda b, j: (j, 0, 0)),
        ],
        out_specs=pl.BlockSpec((1, g, n1, c), lambda b, j: (b, j, 0, 0)),
        out_shape=jax.ShapeDtypeStruct((batch, n2, n1, c), jnp.uint32),
        compiler_params=_cparams(("arbitrary", "arbitrary")),
        name="dft_stage1",
    )(f2, fcat, twc, tws)
    return b, f3


def _rms(z, g):
    ms = jnp.mean(z * z, axis=-1, keepdims=True)
    return z * lax.rsqrt(ms + EPS) * g


def _outproj_kernel(x_ref, a_ref, b_ref, f3_ref, cc_ref, sc_ref, wf_ref, ga_ref, gf_ref, wo_ref,
                    g2_ref, wr_ref, br_ref,
                    x1_ref, h2_ref, info_ref, cnt_ref, rows_ref, uv_ref):
    i = pl.program_id(0)
    n2, kc, _ = b_ref.shape[1:]
    tm = n2 * kc
    attn = a_ref[0].reshape(tm, ATTN_WIDTH).astype(F32)
    r_a = lax.rsqrt(jnp.mean(attn * attn, axis=-1, keepdims=True) + EPS)
    mix_a = jnp.dot((attn * ga_ref[...]).astype(BF16), wo_ref[0:ATTN_WIDTH, :], preferred_element_type=F32)
    _dft_stage3(b_ref, f3_ref, rows_ref, uv_ref)
    n_lane_tiles = FOURIER_WIDTH // LANES
    uvw = jnp.concatenate([uv_ref[jc] for jc in range(n_lane_tiles)], axis=1)
    u = pltpu.unpack_elementwise(uvw, index=0, packed_dtype=BF16, unpacked_dtype=F32).astype(BF16)
    v = pltpu.unpack_elementwise(uvw, index=1, packed_dtype=BF16, unpacked_dtype=F32).astype(BF16)
    cc = cc_ref[...].astype(BF16)
    sc = sc_ref[...].astype(BF16)
    parts = []
    for g in range(N_FOURIER_GROUPS):
        sl = slice(g * FOURIER_GROUP_DIM, (g + 1) * FOURIER_GROUP_DIM)
        parts.append(jnp.dot(u[:, sl], cc, preferred_element_type=F32)
                     + jnp.dot(v[:, sl], sc, preferred_element_type=F32))
    fmix = jnp.concatenate(parts, axis=-1).astype(BF16)
    four = jnp.dot(fmix, wf_ref[...], preferred_element_type=F32)
    r_f = lax.rsqrt(jnp.mean(four * four, axis=-1, keepdims=True) + EPS)
    mix_f = jnp.dot((four * gf_ref[...]).astype(BF16), wo_ref[ATTN_WIDTH:D_MODEL, :],
                    preferred_element_type=F32)
    x1 = x_ref[0].reshape(tm, D_MODEL) + (r_a * mix_a + r_f * mix_f)
    x1_ref[...] = x1
    x1g = x1 * g2_ref[...]
    r_2 = lax.rsqrt(jnp.mean(x1 * x1, axis=-1, keepdims=True) + EPS)
    h2f = x1g * r_2
    h2_ref[...] = pltpu.pack_elementwise([h2f[:, :D_MODEL // 2], h2f[:, D_MODEL // 2:]], packed_dtype=BF16)

    lg = r_2 * jnp.dot(x1g.astype(BF16), wr_ref[...], preferred_element_type=F32) + br_ref[...]
    lane = lax.broadcasted_iota(jnp.int32, lg.shape, 1).astype(F32)
    is_g = lane < N_EXPERT_GROUPS
    gl = jnp.where(is_g, lg, NEG_BIG)
    gmax = jnp.max(gl, axis=-1, keepdims=True)
    gidx = jnp.min(jnp.where(gl == gmax, lane, float(LANES)), axis=-1, keepdims=True)
    gsum = jnp.sum(jnp.where(is_g, jnp.exp(gl - gmax), 0.0), axis=-1, keepdims=True)
    g_w = 1.0 / gsum
    lo_lane = N_EXPERT_GROUPS + EXPERTS_PER_GROUP * gidx
    in_grp = jnp.logical_and(lane >= lo_lane, lane < lo_lane + EXPERTS_PER_GROUP)
    el = jnp.where(in_grp, lg, NEG_BIG)
    m1 = jnp.max(el, axis=-1, keepdims=True)
    i1 = jnp.min(jnp.where(el == m1, lane, float(LANES)), axis=-1, keepdims=True)
    el2 = jnp.where(lane == i1, NEG_BIG, el)
    m2 = jnp.max(el2, axis=-1, keepdims=True)
    i2 = jnp.min(jnp.where(el2 == m2, lane, float(LANES)), axis=-1, keepdims=True)
    r = jnp.exp(m2 - m1)
    w1 = g_w / (1.0 + r)
    w2 = g_w * r / (1.0 + r)
    first_lo = i1 < i2
    ea = jnp.minimum(i1, i2) - lo_lane
    eb = jnp.maximum(i1, i2) - lo_lane
    pair = ea * (7.0 - ea) * 0.5 + (eb - ea - 1.0)
    bucket = gidx * float(N_PAIRS) + pair
    w_lo = jnp.where(first_lo, w1, w2)
    w_hi = jnp.where(first_lo, w2, w1)
    info_ref[...] = jnp.where(lane == 0.0, bucket,
                              jnp.where(lane == 1.0, w_lo, jnp.where(lane == 2.0, w_hi, 0.0)))
    cnt = jnp.sum(jnp.where(lane == bucket, 1.0, 0.0), axis=0, keepdims=True)

    @pl.when(i == 0)
    def _():
        cnt_ref[...] = jnp.zeros(cnt_ref.shape, F32)

    cnt_ref[...] = cnt_ref[...] + cnt


def _out_proj(x, attn, b, f3, cc, sc, w_fmix, ga, gf, w_out, g2, w_router, b_router):
    batch, n, _ = x.shape
    n2 = DFT_N2
    n1 = n // n2
    kc = DFT3_ROWS
    nj = n1 // kc
    tm = n2 * kc
    assert tm == TOKEN_TILE
    t = batch * n
    row = lambda i: (i, 0)
    grp = lambda i: (i // nj, 0, i % nj, 0)
    n_lane_tiles = FOURIER_WIDTH // LANES
    return pl.pallas_call(
        _outproj_kernel,
        grid=(t // tm,),
        in_specs=[
            pl.BlockSpec((1, n2, kc, D_MODEL), grp),
            pl.BlockSpec((1, n2, kc, ATTN_WIDTH), grp),
            pl.BlockSpec((1, n2, kc, FOURIER_WIDTH), grp),
            _const_spec((2 * n2, 2 * n2)),
            _const_spec((FOURIER_GROUP_DIM, FOURIER_GROUP_DIM)),
            _const_spec((FOURIER_GROUP_DIM, FOURIER_GROUP_DIM)),
            _const_spec((FOURIER_WIDTH, FOURIER_WIDTH)),
            _const_spec((1, ATTN_WIDTH)),
            _const_spec((1, FOURIER_WIDTH)),
            _const_spec((D_MODEL, D_MODEL)),
            _const_spec((1, D_MODEL)),
            _const_spec((D_MODEL, LANES)),
            _const_spec((1, LANES)),
        ],
        out_specs=[
            pl.BlockSpec((tm, D_MODEL), row),
            pl.BlockSpec((tm, D_MODEL // 2), row),
            pl.BlockSpec((tm, LANES), row),
            pl.BlockSpec((8, LANES), lambda i: (0, 0)),
        ],
        out_shape=[
            jax.ShapeDtypeStruct((t, D_MODEL), F32),
            jax.ShapeDtypeStruct((t, D_MODEL // 2), jnp.uint32),
            jax.ShapeDtypeStruct((t, LANES), F32),
            jax.ShapeDtypeStruct((8, LANES), F32),
        ],
        scratch_shapes=[
            pltpu.VMEM((n_lane_tiles, tm, LANES), jnp.uint32),
            pltpu.VMEM((n_lane_tiles, tm, LANES), jnp.uint32),
        ],
        compiler_params=_cparams(("arbitrary",)),
        name="out_proj_router",
    )(x.reshape(batch, n2, n1, D_MODEL), attn.reshape(batch, n2, n1, ATTN_WIDTH), b, f3,
      cc, sc, w_fmix, ga, gf, w_out, g2, w_router, b_router)


def _rank_kernel(info_ref, start_ref, pos_ref, base_ref):
    i = pl.program_id(0)
    tm = info_ref.shape[0]

    @pl.when(i == 0)
    def _():
        base_ref[...] = start_ref[...]

    bucket = info_ref[...].T[0:1, :]
    brow = lax.broadcasted_iota(jnp.int32, (32, tm), 0).astype(F32)
    onehot = jnp.where(brow == bucket, 1.0, 0.0)
    oh = onehot.astype(BF16)
    src = lax.broadcasted_iota(jnp.int32, (tm, tm), 0)
    dst = lax.broadcasted_iota(jnp.int32, (tm, tm), 1)
    upper = jnp.where(src <= dst, 1.0, 0.0).astype(BF16)
    cum = jnp.dot(oh, upper, preferred_element_type=F32)
    base = base_ref[...]
    base_t = jnp.concatenate([base] * (tm // LANES), axis=1)
    pos = jnp.sum(onehot * (cum - 1.0 + base_t), axis=0, keepdims=True)
    pos_ref[0] = pos.astype(jnp.int32)
    ones = jnp.ones((tm, LANES), BF16)
    base_ref[...] = base + jnp.dot(oh, ones, preferred_element_type=F32)


def _rank(info, starts):
    t = info.shape[0]
    tm = TOKEN_TILE
    return pl.pallas_call(
        _rank_kernel,
        grid=(t // tm,),
        in_specs=[
            pl.BlockSpec((tm, LANES), lambda i: (i, 0)),
            pl.BlockSpec((32, LANES), lambda i: (0, 0)),
        ],
        out_specs=pl.BlockSpec((1, 1, tm), lambda i: (i, 0, 0)),
        out_shape=jax.ShapeDtypeStruct((t // tm, 1, tm), jnp.int32),
        scratch_shapes=[pltpu.VMEM((32, LANES), F32)],
        compiler_params=_cparams(("arbitrary",)),
        name="rank",
    )(info, starts)


def _for_each_row(n_rows, fn):
    def body(c, carry):
        for u in range(ROW_DMA_UNROLL):
            fn(c * ROW_DMA_UNROLL + u)
        return carry

    lax.fori_loop(0, n_rows // ROW_DMA_UNROLL, body, 0)


def _scatter_kernel(pad_ref, pos_ref, ha_ref, hb_ref, xs_ref, zero_ref, sem, zsem, *, steps_a, n_steps,
                    n_tiles):
    i = pl.program_id(0)
    tm = ha_ref.shape[0]
    r_tile = zero_ref.shape[0]

    @pl.when(i == 0)
    def _():
        zero = jnp.zeros(zero_ref.shape, F32)
        zero_ref[...] = pltpu.pack_elementwise([zero, zero], packed_dtype=BF16)

        def tile_copy(t):
            return pltpu.make_async_copy(zero_ref, xs_ref.at[pl.ds(pl.multiple_of(t * r_tile, r_tile), r_tile)],
                                         zsem)

        def start_tile(t, carry):
            tile_copy(t).start()
            return carry

        def wait_tile(t, carry):
            tile_copy(t).wait()
            return carry

        for b in range(N_BUCKETS):
            lax.fori_loop(pad_ref[2 * b], pad_ref[2 * b + 1], start_tile, 0)
        lax.fori_loop(pad_ref[2 * N_BUCKETS], n_tiles, start_tile, 0)
        for b in range(N_BUCKETS):
            lax.fori_loop(pad_ref[2 * b], pad_ref[2 * b + 1], wait_tile, 0)
        lax.fori_loop(pad_ref[2 * N_BUCKETS], n_tiles, wait_tile, 0)

    def scatter_from(h_ref):
        def row_copy(r):
            return pltpu.make_async_copy(h_ref.at[pl.ds(r, 1)], xs_ref.at[pl.ds(pos_ref[0, 0, r], 1)], sem)

        _for_each_row(tm, lambda r: row_copy(r).start())
        _for_each_row(tm, lambda r: row_copy(r).wait())

    @pl.when(i < steps_a)
    def _():
        scatter_from(ha_ref)

    @pl.when(i >= steps_a)
    def _():
        scatter_from(hb_ref)


def _scatter(pad, pos, h_a, h_b, n_tiles):
    w = h_a.shape[1]
    tm = TOKEN_TILE
    steps_a = h_a.shape[0] // tm
    n_steps = steps_a + h_b.shape[0] // tm
    r = EXPERT_TILE
    return pl.pallas_call(
        functools.partial(_scatter_kernel, steps_a=steps_a, n_steps=n_steps, n_tiles=n_tiles),
        grid_spec=pltpu.PrefetchScalarGridSpec(
            num_scalar_prefetch=1,
            grid=(n_steps,),
            in_specs=[
                pl.BlockSpec((1, 1, tm), lambda i, pad: (i, 0, 0), memory_space=pltpu.SMEM),
                pl.BlockSpec((tm, w), lambda i, pad: (jnp.minimum(i, steps_a - 1), 0)),
                pl.BlockSpec((tm, w), lambda i, pad: (jnp.maximum(i - steps_a, 0), 0)),
            ],
            out_specs=pl.BlockSpec(memory_space=pl.ANY),
            scratch_shapes=[pltpu.VMEM((r, w), jnp.uint32), pltpu.SemaphoreType.DMA,
                            pltpu.SemaphoreType.DMA],
        ),
        out_shape=jax.ShapeDtypeStruct((n_tiles * r, w), jnp.uint32),
        compiler_params=_cparams(("arbitrary",)),
        name="scatter_rows",
    )(pad, pos, h_a, h_b)


def _expert_kernel(lo_ref, hi_ref, blk_ref, valid_ref, x_ref,
                   wg0_ref, wu0_ref, wd0_ref, wg1_ref, wu1_ref, wd1_ref, o_ref):
    del lo_ref, hi_ref, blk_ref
    i = pl.program_id(0)

    @pl.when(valid_ref[i] == 1)
    def _():
        xw = x_ref[...]
        x = jnp.concatenate(
            [pltpu.unpack_elementwise(xw, index=0, packed_dtype=BF16, unpacked_dtype=F32),
             pltpu.unpack_elementwise(xw, index=1, packed_dtype=BF16, unpacked_dtype=F32)],
            axis=1).astype(BF16)

        def mlp(wg_ref, wu_ref, wd_ref):
            a = jnp.dot(x, wg_ref[0], preferred_element_type=F32)
            u = jnp.dot(x, wu_ref[0], preferred_element_type=F32)
            h = (a * (1.0 / (1.0 + jnp.exp(-a))) * u).astype(BF16)
            return jnp.dot(h, wd_ref[0], preferred_element_type=F32)

        e_lo = mlp(wg0_ref, wu0_ref, wd0_ref)
        e_hi = mlp(wg1_ref, wu1_ref, wd1_ref)
        o_ref[...] = pltpu.pack_elementwise([e_lo, e_hi], packed_dtype=BF16)

    @pl.when(valid_ref[i] == 0)
    def _():
        zero = jnp.zeros(o_ref.shape, F32)
        o_ref[...] = pltpu.pack_elementwise([zero, zero], packed_dtype=BF16)


def _experts(lo, hi, blk, valid, xs, wg, wu, wd):
    n_tiles = lo.shape[0]
    r = EXPERT_TILE
    x_map = lambda i, lo, hi, blk, valid: (blk[i], 0)
    lo_map = lambda i, lo, hi, blk, valid: (lo[i], 0, 0)
    hi_map = lambda i, lo, hi, blk, valid: (hi[i], 0, 0)
    one = pl.Buffered(1)
    w_up = (1, D_MODEL, EXPERT_FF)
    w_dn = (1, EXPERT_FF, D_MODEL)
    return pl.pallas_call(
        _expert_kernel,
        grid_spec=pltpu.PrefetchScalarGridSpec(
            num_scalar_prefetch=4,
            grid=(n_tiles,),
            in_specs=[
                pl.BlockSpec((r, D_MODEL // 2), x_map),
                pl.BlockSpec(w_up, lo_map, pipeline_mode=one),
                pl.BlockSpec(w_up, lo_map, pipeline_mode=one),
                pl.BlockSpec(w_dn, lo_map, pipeline_mode=one),
                pl.BlockSpec(w_up, hi_map),
                pl.BlockSpec(w_up, hi_map),
                pl.BlockSpec(w_dn, hi_map),
            ],
            out_specs=pl.BlockSpec((r, D_MODEL), lambda i, lo, hi, blk, valid: (i, 0)),
        ),
        out_shape=jax.ShapeDtypeStruct((n_tiles * r, D_MODEL), jnp.uint32),
        compiler_params=_cparams(("arbitrary",)),
        name="experts",
    )(lo, hi, blk, valid, xs, wg, wu, wd, wg, wu, wd)


def _combine_kernel(pos_ref, nxt_ref, x1_ref, info_ref, eo_ref, y_ref, gbuf, sems, *, n_steps):
    i = pl.program_id(0)
    tm = x1_ref.shape[0]
    slot = lax.rem(i, 2)

    def row_copy(p_ref, s, r):
        return pltpu.make_async_copy(eo_ref.at[pl.ds(p_ref[0, 0, r], 1)], gbuf.at[s, pl.ds(r, 1)],
                                     sems.at[s])

    @pl.when(i == 0)
    def _():
        _for_each_row(tm, lambda r: row_copy(pos_ref, 0, r).start())

    @pl.when(i + 1 < n_steps)
    def _():
        _for_each_row(tm, lambda r: row_copy(nxt_ref, 1 - slot, r).start())

    _for_each_row(tm, lambda r: row_copy(pos_ref, slot, r).wait())

    g = gbuf[slot]
    e_lo = pltpu.unpack_elementwise(g, index=0, packed_dtype=BF16, unpacked_dtype=F32)
    e_hi = pltpu.unpack_elementwise(g, index=1, packed_dtype=BF16, unpacked_dtype=F32)
    info = info_ref[...]
    y = x1_ref[...] + info[:, 1:2] * e_lo + info[:, 2:3] * e_hi
    y_ref[0] = y.reshape(y_ref.shape[1:])


def _combine(pos, x1, info, eo, batch, n):
    n2 = DFT_N2
    n1 = n // n2
    kc = DFT3_ROWS
    nj = n1 // kc
    tm = n2 * kc
    n_steps = batch * nj
    return pl.pallas_call(
        functools.partial(_combine_kernel, n_steps=n_steps),
        grid=(n_steps,),
        in_specs=[
            pl.BlockSpec((1, 1, tm), lambda i: (i, 0, 0), memory_space=pltpu.SMEM),
            pl.BlockSpec((1, 1, tm), lambda i: (jnp.minimum(i + 1, n_steps - 1), 0, 0),
                         memory_space=pltpu.SMEM),
            pl.BlockSpec((tm, D_MODEL), lambda i: (i, 0)),
            pl.BlockSpec((tm, LANES), lambda i: (i, 0)),
            pl.BlockSpec(memory_space=pl.ANY),
        ],
        out_specs=pl.BlockSpec((1, n2, kc, D_MODEL), lambda i: (i // nj, 0, i % nj, 0)),
        out_shape=jax.ShapeDtypeStruct((batch, n2, n1, D_MODEL), F32),
        scratch_shapes=[pltpu.VMEM((2, tm, D_MODEL), jnp.uint32), pltpu.SemaphoreType.DMA((2,))],
        compiler_params=_cparams(("arbitrary",)),
        name="combine",
    )(pos, pos, x1, info, eo)


def _rope_tables(n):
    rows = n // GRID_W
    row_id = np.repeat(np.arange(rows, dtype=np.float64), GRID_W)
    col_id = np.tile(np.arange(GRID_W, dtype=np.float64), rows)
    inv_freq = np.exp(-math.log(ROPE_THETA) * 2.0 * np.arange(ROPE_FREQS, dtype=np.float64) / (2 * ROPE_FREQS))
    ar = row_id[:, None] * inv_freq
    ac = col_id[:, None] * inv_freq
    cr, sr, ccol, scol = np.cos(ar), np.sin(ar), np.cos(ac), np.sin(ac)
    zero = np.zeros_like(sr)
    c = np.concatenate([cr, cr, ccol, ccol], axis=1)
    s1 = np.concatenate([-sr, zero, -scol, zero], axis=1)
    s2 = np.concatenate([zero, sr, zero, scol], axis=1)
    return jnp.asarray(c, F32), jnp.asarray(s1, F32), jnp.asarray(s2, F32)


def _channel_dft_tables():
    g = FOURIER_GROUP_DIM
    a = 2.0 * np.pi * ((np.arange(g)[:, None] * np.arange(g)[None, :]) % g) / g
    return jnp.asarray(np.cos(a), F32), jnp.asarray(np.sin(a), F32)


_PAIR_LO = np.array([0, 0, 0, 1, 1, 2], np.int32)
_PAIR_HI = np.array([1, 2, 3, 2, 3, 3], np.int32)


def kernel(x_prompt, x_sample, norm1_g, w_in, q_norm_g, k_norm_g, w_fmix, attn_out_g, fourier_out_g,
           w_out, norm2_g, w_router_group, b_router_group, w_router_expert, b_router_expert,
           w_gate, w_up, w_down):
    l = 0
    g1 = norm1_g[l][None, :]
    g2 = norm2_g[l][None, :]
    qg = q_norm_g[l][None, :]
    kg = k_norm_g[l][None, :]
    ga = attn_out_g[l][None, :]
    gf = fourier_out_g[l][None, :]
    w_in_b = w_in[l].astype(BF16)
    w_fmix_b = w_fmix[l].astype(BF16)
    w_out_b = w_out[l].astype(BF16)
    wg_b = w_gate[l].astype(BF16)
    wu_b = w_up[l].astype(BF16)
    wd_b = w_down[l].astype(BF16)
    n_r = N_EXPERT_GROUPS + N_EXPERTS
    w_router = jnp.zeros((D_MODEL, LANES), F32)
    w_router = w_router.at[:, :N_EXPERT_GROUPS].set(w_router_group[l])
    w_router = w_router.at[:, N_EXPERT_GROUPS:n_r].set(w_router_expert[l]).astype(BF16)
    b_router = jnp.zeros((1, LANES), F32)
    b_router = b_router.at[0, :N_EXPERT_GROUPS].set(b_router_group[l])
    b_router = b_router.at[0, N_EXPERT_GROUPS:n_r].set(b_router_expert[l])
    cc, sc = _channel_dft_tables()
    score_bound = _score_bound(qg, kg)

    seqs = []
    for x in (x_prompt, x_sample):
        batch, n, _ = x.shape
        x2d = x.reshape(batch * n, D_MODEL)
        rope_c, rope_s1, rope_s2 = _rope_tables(n)
        q, k, vt, f2 = _in_proj(x2d, n, g1, w_in_b, qg, kg, rope_c, rope_s1, rope_s2)
        attn = _attention(q.reshape(batch, n, ATTN_WIDTH), k.reshape(batch, n, KV_WIDTH), vt, score_bound)
        b, f3 = _dft_stage1(f2, n)
        x1, h2, info, cnt = _out_proj(x, attn, b, f3, cc, sc, w_fmix_b, ga, gf, w_out_b, g2, w_router, b_router)
        seqs.append((x.shape, x1, h2, info, cnt))

    r = EXPERT_TILE
    t_all = sum(s[1].shape[0] for s in seqs)
    n_tiles = t_all // r + N_BUCKETS
    counts = sum(s[4][0, :N_BUCKETS] for s in seqs).astype(jnp.int32)
    tiles_b = (counts + (r - 1)) // r
    tile_end = jnp.cumsum(tiles_b)
    tile_start = tile_end - tiles_b
    total_tiles = tile_end[-1]
    starts = jnp.zeros((32,), F32).at[:N_BUCKETS].set((tile_start * r).astype(F32))
    starts = jnp.broadcast_to(starts[:, None], (32, LANES))
    tile_id = jnp.arange(n_tiles, dtype=jnp.int32)
    blk = jnp.minimum(tile_id, total_tiles - 1)
    tile_bucket = jnp.sum((tile_end[None, :] <= blk[:, None]).astype(jnp.int32), axis=1)
    grp = tile_bucket // N_PAIRS
    pair = tile_bucket % N_PAIRS
    lo = grp * EXPERTS_PER_GROUP + jnp.asarray(_PAIR_LO)[pair]
    hi = grp * EXPERTS_PER_GROUP + jnp.asarray(_PAIR_HI)[pair]
    valid = (tile_id < total_tiles).astype(jnp.int32)

    info_all = jnp.concatenate([s[3] for s in seqs], axis=0)
    pos = _rank(info_all, starts)

    has = (tiles_b > 0).astype(jnp.int32)
    pad = jnp.concatenate([jnp.stack([(tile_end - 1) * has, tile_end * has], axis=1).reshape(-1),
                           total_tiles[None]]).astype(jnp.int32)
    xs = _scatter(pad, pos, seqs[0][2], seqs[1][2], n_tiles)
    eo = _experts(lo, hi, blk, valid, xs, wg_b, wu_b, wd_b)

    outs = []
    off = 0
    for shape, x1, _, info, _ in seqs:
        nt = x1.shape[0] // TOKEN_TILE
        y = _combine(pos[off:off + nt], x1, info, eo, shape[0], shape[1])
        outs.append(y.reshape(shape))
        off += nt
    return tuple(outs)
```

```python
import functools
import math

import numpy as np
import jax
import jax.numpy as jnp
from jax import lax
from jax.experimental import pallas as pl
from jax.experimental.pallas import tpu as pltpu

F32 = jnp.float32
BF16 = jnp.bfloat16

EPS = 1e-6
D_MODEL = 2048
HEAD_DIM = 128
N_HEADS = 8
N_KV_HEADS = 2
Q_PER_KV = N_HEADS // N_KV_HEADS
ATTN_WIDTH = N_HEADS * HEAD_DIM
KV_WIDTH = N_KV_HEADS * HEAD_DIM
FOURIER_WIDTH = D_MODEL - ATTN_WIDTH
N_FOURIER_GROUPS = 4
FOURIER_GROUP_DIM = FOURIER_WIDTH // N_FOURIER_GROUPS
IN_WIDTH = ATTN_WIDTH + 2 * KV_WIDTH + FOURIER_WIDTH
GRID_W = 64
ROPE_THETA = 10000.0
ROPE_FREQS = HEAD_DIM // 4
N_EXPERT_GROUPS = 4
EXPERTS_PER_GROUP = 4
N_EXPERTS = N_EXPERT_GROUPS * EXPERTS_PER_GROUP
EXPERT_FF = D_MODEL // 2
N_PAIRS = 6
N_BUCKETS = N_EXPERT_GROUPS * N_PAIRS

LANES = 128
TOKEN_TILE = 512
ATTN_TILE = 1024
EXPERT_TILE = 256
DFT_N2 = 32
DFT1_ROWS = 512
DFT3_ROWS = 16
ROW_DMA_UNROLL = 8
VMEM_LIMIT = 56 * 1024 * 1024
NEG_BIG = -1e30
LOG2E = 1.4426950408889634
SCORE_HEADROOM = 64
SCORE_BOUND_MAX = 87.0


def _cparams(sem, vmem=VMEM_LIMIT):
    return pltpu.CompilerParams(dimension_semantics=sem, vmem_limit_bytes=vmem)


def _const_spec(shape):
    nd = len(shape)
    return pl.BlockSpec(shape, lambda *_: (0,) * nd, pipeline_mode=pl.Buffered(1))


def _inproj_kernel(x_ref, g1_ref, w_ref, qg_ref, kg_ref, c_ref, s1_ref, s2_ref, perm_ref,
                   q_ref, k_ref, vt_ref, f_ref, *, q_scale):
    x = x_ref[...]
    r = lax.rsqrt(jnp.mean(x * x, axis=-1, keepdims=True) + EPS)
    h = (x * g1_ref[...]).astype(BF16)
    c = c_ref[...]
    s1 = s1_ref[...]
    s2 = s2_ref[...]

    def norm_rope(z, g):
        zms = jnp.mean(z * z, axis=-1, keepdims=True)
        zn = z * lax.rsqrt(zms + EPS) * g
        return zn * c + pltpu.roll(zn, 96, 1) * s1 + pltpu.roll(zn, 32, 1) * s2

    qg = qg_ref[...]
    kg = kg_ref[...]
    pq = r * jnp.dot(h, w_ref[:, 0:ATTN_WIDTH], preferred_element_type=F32)
    for hh in range(N_HEADS):
        sl = slice(hh * HEAD_DIM, (hh + 1) * HEAD_DIM)
        q_ref[:, sl] = (norm_rope(pq[:, sl], qg) * q_scale).astype(BF16)
    pk = r * jnp.dot(h, w_ref[:, ATTN_WIDTH:ATTN_WIDTH + KV_WIDTH], preferred_element_type=F32)
    for hh in range(N_KV_HEADS):
        sl = slice(hh * HEAD_DIM, (hh + 1) * HEAD_DIM)
        k_ref[:, sl] = norm_rope(pk[:, sl], kg).astype(BF16)
    pv = r * jnp.dot(h, w_ref[:, ATTN_WIDTH + KV_WIDTH:ATTN_WIDTH + 2 * KV_WIDTH],
                     preferred_element_type=F32)
    vt_ref[0] = pv.T.astype(BF16)
    pf = r * jnp.dot(h, w_ref[:, ATTN_WIDTH + 2 * KV_WIDTH:IN_WIDTH], preferred_element_type=F32)
    pfp = jnp.dot(perm_ref[...], pf.astype(BF16), preferred_element_type=F32).astype(BF16)
    rows = pfp.shape[0] // DFT_N2
    for t2 in range(DFT_N2):
        f_ref[0, :, t2 * FOURIER_WIDTH:(t2 + 1) * FOURIER_WIDTH] = pfp[t2 * rows:(t2 + 1) * rows, :]


def _in_proj(x2d, n, g1, w_in, qg, kg, rope_c, rope_s1, rope_s2):
    t = x2d.shape[0]
    tm = TOKEN_TILE
    nt = n // tm
    batch = t // n
    q_scale = LOG2E / math.sqrt(HEAD_DIM)
    r_out = np.arange(tm)
    perm = np.zeros((tm, tm), np.float32)
    perm[r_out, (r_out % (tm // DFT_N2)) * DFT_N2 + r_out // (tm // DFT_N2)] = 1.0
    perm = jnp.asarray(perm, BF16)
    return pl.pallas_call(
        functools.partial(_inproj_kernel, q_scale=q_scale),
        grid=(t // tm,),
        in_specs=[
            pl.BlockSpec((tm, D_MODEL), lambda i: (i, 0)),
            _const_spec((1, D_MODEL)),
            _const_spec((D_MODEL, IN_WIDTH)),
            _const_spec((1, HEAD_DIM)),
            _const_spec((1, HEAD_DIM)),
            pl.BlockSpec((tm, HEAD_DIM), lambda i: (i % nt, 0)),
            pl.BlockSpec((tm, HEAD_DIM), lambda i: (i % nt, 0)),
            pl.BlockSpec((tm, HEAD_DIM), lambda i: (i % nt, 0)),
            _const_spec((tm, tm)),
        ],
        out_specs=[
            pl.BlockSpec((tm, ATTN_WIDTH), lambda i: (i, 0)),
            pl.BlockSpec((tm, KV_WIDTH), lambda i: (i, 0)),
            pl.BlockSpec((1, KV_WIDTH, tm), lambda i: (i // nt, 0, i % nt)),
            pl.BlockSpec((1, tm // DFT_N2, DFT_N2 * FOURIER_WIDTH), lambda i: (i // nt, i % nt, 0)),
        ],
        out_shape=[
            jax.ShapeDtypeStruct((t, ATTN_WIDTH), BF16),
            jax.ShapeDtypeStruct((t, KV_WIDTH), BF16),
            jax.ShapeDtypeStruct((batch, KV_WIDTH, n), BF16),
            jax.ShapeDtypeStruct((batch, n // DFT_N2, DFT_N2 * FOURIER_WIDTH), BF16),
        ],
        compiler_params=_cparams(("arbitrary",)),
        name="in_proj",
    )(x2d, g1, w_in, qg, kg, rope_c, rope_s1, rope_s2, perm)


def _attn_kernel(sb_ref, q_ref, k_ref, vt_ref, o_ref, m_ref, l_ref, acc_ref, *, nk):
    ki = pl.program_id(3)
    bounded = sb_ref[0] == 1

    @pl.when(ki == 0)
    def _():
        m_ref[...] = jnp.full(m_ref.shape, NEG_BIG, F32)
        l_ref[...] = jnp.zeros(l_ref.shape, F32)
        acc_ref[...] = jnp.zeros(acc_ref.shape, F32)

    def scores(g):
        q = q_ref[0, :, g * HEAD_DIM:(g + 1) * HEAD_DIM]
        return lax.dot_general(k_ref[0], q, (((1,), (1,)), ((), ())), preferred_element_type=F32)

    @pl.when(bounded)
    def _():
        shift = sb_ref[1].astype(F32)
        for g in range(Q_PER_KV):
            p = jnp.exp2(scores(g) - shift)
            l_ref[g] = l_ref[g] + jnp.sum(p, axis=0, keepdims=True)
            acc_ref[g] = acc_ref[g] + jnp.dot(vt_ref[0], p.astype(BF16), preferred_element_type=F32)

    @pl.when(jnp.logical_not(bounded))
    def _():
        for g in range(Q_PER_KV):
            s = scores(g)
            m_prev = m_ref[g]
            m_new = jnp.maximum(m_prev, jnp.max(s, axis=0, keepdims=True))
            alpha = jnp.exp2(m_prev - m_new)
            p = jnp.exp2(s - m_new)
            l_ref[g] = alpha * l_ref[g] + jnp.sum(p, axis=0, keepdims=True)
            acc_ref[g] = alpha * acc_ref[g] + jnp.dot(vt_ref[0], p.astype(BF16),
                                                      preferred_element_type=F32)
            m_ref[g] = m_new

    @pl.when(ki == nk - 1)
    def _():
        for g in range(Q_PER_KV):
            o = acc_ref[g] * (1.0 / l_ref[g])
            o_ref[0, :, g * HEAD_DIM:(g + 1) * HEAD_DIM] = o.T.astype(BF16)


def _score_bound(qg, kg):
    ub = (HEAD_DIM * jnp.max(jnp.abs(qg)) * jnp.max(jnp.abs(kg))
          * (LOG2E / math.sqrt(HEAD_DIM)) * (1.0 + 1e-3))
    ok = ub <= SCORE_BOUND_MAX
    shift = jnp.where(ok, jnp.ceil(ub) - SCORE_HEADROOM, 0.0)
    return jnp.stack([ok.astype(jnp.int32), shift.astype(jnp.int32)])


def _attention(q, k, vt, score_bound):
    batch, n, _ = q.shape
    tq = min(2 * ATTN_TILE, n)
    tk = min(ATTN_TILE, n)
    gw = Q_PER_KV * HEAD_DIM
    return pl.pallas_call(
        functools.partial(_attn_kernel, nk=n // tk),
        grid_spec=pltpu.PrefetchScalarGridSpec(
            num_scalar_prefetch=1,
            grid=(batch, N_KV_HEADS, n // tq, n // tk),
            in_specs=[
                pl.BlockSpec((1, tq, gw), lambda b, h, qi, ki, sb: (b, qi, h)),
                pl.BlockSpec((1, tk, HEAD_DIM), lambda b, h, qi, ki, sb: (b, ki, h)),
                pl.BlockSpec((1, HEAD_DIM, tk), lambda b, h, qi, ki, sb: (b, h, ki)),
            ],
            out_specs=pl.BlockSpec((1, tq, gw), lambda b, h, qi, ki, sb: (b, qi, h)),
            scratch_shapes=[
                pltpu.VMEM((Q_PER_KV, 1, tq), F32),
                pltpu.VMEM((Q_PER_KV, 1, tq), F32),
                pltpu.VMEM((Q_PER_KV, HEAD_DIM, tq), F32),
            ],
        ),
        out_shape=jax.ShapeDtypeStruct((batch, n, ATTN_WIDTH), BF16),
        compiler_params=_cparams(("arbitrary", "arbitrary", "arbitrary", "arbitrary")),
        name="attention",
    )(score_bound, q, k, vt)


def _dft1_kernel(f_ref, fc_ref, twc_ref, tws_ref, b_ref, *, n1):
    x = f_ref[0]
    a = jnp.dot(fc_ref[...].astype(BF16), x, preferred_element_type=F32)
    ar = a[:n1]
    ai = a[n1:]
    width = b_ref.shape[3]
    for t in range(b_ref.shape[1]):
        c = twc_ref[t]
        s = tws_ref[t]
        for j in range(width // LANES):
            sl = slice(t * width + j * LANES, t * width + (j + 1) * LANES)
            br = ar[:, sl] * c + ai[:, sl] * s
            bi = ai[:, sl] * c - ar[:, sl] * s
            b_ref[0, t, :, j * LANES:(j + 1) * LANES] = pltpu.pack_elementwise([br, bi], packed_dtype=BF16)


def _dft_stage3(b_ref, f3_ref, rows_ref, uv_ref):
    n2, kc, c = b_ref.shape[1:]
    n_lane_tiles = c // LANES
    for jc in range(n_lane_tiles):
        rows_ref[jc] = b_ref[0, :, :, jc * LANES:(jc + 1) * LANES].reshape(n2 * kc, LANES)
    f3 = f3_ref[...].astype(BF16)
    for j in range(kc):
        w = jnp.concatenate([rows_ref[jc, pl.ds(j, n2, stride=kc), :] for jc in range(n_lane_tiles)],
                            axis=1)
        br = pltpu.unpack_elementwise(w, index=0, packed_dtype=BF16, unpacked_dtype=F32)
        bi = pltpu.unpack_elementwise(w, index=1, packed_dtype=BF16, unpacked_dtype=F32)
        rhs = jnp.concatenate([br, bi], axis=0).astype(BF16)
        uv = jnp.dot(f3, rhs, preferred_element_type=F32)
        packed = pltpu.pack_elementwise([uv[:n2], uv[n2:]], packed_dtype=BF16)
        for jc in range(n_lane_tiles):
            uv_ref[jc, pl.ds(j, n2, stride=kc), :] = packed[:, jc * LANES:(jc + 1) * LANES]


def _dft_tables(n):
    n2 = DFT_N2
    n1 = n // n2
    scale = 1.0 / math.sqrt(n * FOURIER_GROUP_DIM)
    k1 = np.arange(n1)[:, None]
    a1 = 2.0 * np.pi * ((k1 * np.arange(n1)[None, :]) % n1) / n1
    fcat = np.concatenate([np.cos(a1), -np.sin(a1)], axis=0) * scale
    th = 2.0 * np.pi * ((np.arange(n2)[:, None] * k1.T) % n) / n
    twc = np.repeat(np.cos(th)[:, :, None], LANES, axis=2)
    tws = np.repeat(np.sin(th)[:, :, None], LANES, axis=2)
    a3 = 2.0 * np.pi * ((np.arange(n2)[:, None] * np.arange(n2)[None, :]) % n2) / n2
    c3, s3 = np.cos(a3), np.sin(a3)
    f3 = np.block([[c3, s3], [-s3, c3]])
    return jnp.asarray(fcat, F32), jnp.asarray(twc, F32), jnp.asarray(tws, F32), jnp.asarray(f3, F32)


def _dft_stage1(f2, n):
    batch = f2.shape[0]
    c = FOURIER_WIDTH
    n2 = DFT_N2
    n1 = n // n2
    g = min(n2, DFT1_ROWS // n1)
    fcat, twc, tws, f3 = _dft_tables(n)
    b = pl.pallas_call(
        functools.partial(_dft1_kernel, n1=n1),
        grid=(batch, n2 // g),
        in_specs=[
            pl.BlockSpec((1, n1, g * c), lambda b, j: (b, 0, j)),
            _const_spec((2 * n1, n1)),
            pl.BlockSpec((g, n1, LANES), lambda b, j: (j, 0, 0)),
            pl.BlockSpec((g, n1, LANES), lambda b, j: (j, 0, 0)),
        ],
        out_specs=pl.BlockSpec((1, g, n1, c), lambda b, j: (b, j, 0, 0)),
        out_shape=jax.ShapeDtypeStruct((batch, n2, n1, c), jnp.uint32),
        compiler_params=_cparams(("arbitrary", "arbitrary")),
        name="dft_stage1",
    )(f2, fcat, twc, tws)
    return b, f3


def _rms(z, g):
    ms = jnp.mean(z * z, axis=-1, keepdims=True)
    return z * lax.rsqrt(ms + EPS) * g


def _outproj_kernel(x_ref, a_ref, b_ref, f3_ref, cc_ref, sc_ref, wf_ref, ga_ref, gf_ref, wo_ref,
                    g2_ref, wr_ref, br_ref,
                    x1_ref, h2_ref, info_ref, cnt_ref, rows_ref, uv_ref):
    i = pl.program_id(0)
    n2, kc, _ = b_ref.shape[1:]
    tm = n2 * kc
    attn = a_ref[0].reshape(tm, ATTN_WIDTH).astype(F32)
    r_a = lax.rsqrt(jnp.mean(attn * attn, axis=-1, keepdims=True) + EPS)
    mix_a = jnp.dot((attn * ga_ref[...]).astype(BF16), wo_ref[0:ATTN_WIDTH, :], preferred_element_type=F32)
    _dft_stage3(b_ref, f3_ref, rows_ref, uv_ref)
    n_lane_tiles = FOURIER_WIDTH // LANES
    uvw = jnp.concatenate([uv_ref[jc] for jc in range(n_lane_tiles)], axis=1)
    u = pltpu.unpack_elementwise(uvw, index=0, packed_dtype=BF16, unpacked_dtype=F32).astype(BF16)
    v = pltpu.unpack_elementwise(uvw, index=1, packed_dtype=BF16, unpacked_dtype=F32).astype(BF16)
    cc = cc_ref[...].astype(BF16)
    sc = sc_ref[...].astype(BF16)
    parts = []
    for g in range(N_FOURIER_GROUPS):
        sl = slice(g * FOURIER_GROUP_DIM, (g + 1) * FOURIER_GROUP_DIM)
        parts.append(jnp.dot(u[:, sl], cc, preferred_element_type=F32)
                     + jnp.dot(v[:, sl], sc, preferred_element_type=F32))
    fmix = jnp.concatenate(parts, axis=-1).astype(BF16)
    four = jnp.dot(fmix, wf_ref[...], preferred_element_type=F32)
    r_f = lax.rsqrt(jnp.mean(four * four, axis=-1, keepdims=True) + EPS)
    mix_f = jnp.dot((four * gf_ref[...]).astype(BF16), wo_ref[ATTN_WIDTH:D_MODEL, :],
                    preferred_element_type=F32)
    x1 = x_ref[0].reshape(tm, D_MODEL) + (r_a * mix_a + r_f * mix_f)
    x1_ref[...] = x1
    x1g = x1 * g2_ref[...]
    r_2 = lax.rsqrt(jnp.mean(x1 * x1, axis=-1, keepdims=True) + EPS)
    h2f = x1g * r_2
    h2_ref[...] = pltpu.pack_elementwise([h2f[:, :D_MODEL // 2], h2f[:, D_MODEL // 2:]], packed_dtype=BF16)

    lg = r_2 * jnp.dot(x1g.astype(BF16), wr_ref[...], preferred_element_type=F32) + br_ref[...]
    lane = lax.broadcasted_iota(jnp.int32, lg.shape, 1).astype(F32)
    is_g = lane < N_EXPERT_GROUPS
    gl = jnp.where(is_g, lg, NEG_BIG)
    gmax = jnp.max(gl, axis=-1, keepdims=True)
    gidx = jnp.min(jnp.where(gl == gmax, lane, float(LANES)), axis=-1, keepdims=True)
    gsum = jnp.sum(jnp.where(is_g, jnp.exp(gl - gmax), 0.0), axis=-1, keepdims=True)
    g_w = 1.0 / gsum
    lo_lane = N_EXPERT_GROUPS + EXPERTS_PER_GROUP * gidx
    in_grp = jnp.logical_and(lane >= lo_lane, lane < lo_lane + EXPERTS_PER_GROUP)
    el = jnp.where(in_grp, lg, NEG_BIG)
    m1 = jnp.max(el, axis=-1, keepdims=True)
    i1 = jnp.min(jnp.where(el == m1, lane, float(LANES)), axis=-1, keepdims=True)
    el2 = jnp.where(lane == i1, NEG_BIG, el)
    m2 = jnp.max(el2, axis=-1, keepdims=True)
    i2 = jnp.min(jnp.where(el2 == m2, lane, float(LANES)), axis=-1, keepdims=True)
    r = jnp.exp(m2 - m1)
    w1 = g_w / (1.0 + r)
    w2 = g_w * r / (1.0 + r)
    first_lo = i1 < i2
    ea = jnp.minimum(i1, i2) - lo_lane
    eb = jnp.maximum(i1, i2) - lo_lane
    pair = ea * (7.0 - ea) * 0.5 + (eb - ea - 1.0)
    bucket = gidx * float(N_PAIRS) + pair
    w_lo = jnp.where(first_lo, w1, w2)
    w_hi = jnp.where(first_lo, w2, w1)
    info_ref[...] = jnp.where(lane == 0.0, bucket,
                              jnp.where(lane == 1.0, w_lo, jnp.where(lane == 2.0, w_hi, 0.0)))
    cnt = jnp.sum(jnp.where(lane == bucket, 1.0, 0.0), axis=0, keepdims=True)

    @pl.when(i == 0)
    def _():
        cnt_ref[...] = jnp.zeros(cnt_ref.shape, F32)

    cnt_ref[...] = cnt_ref[...] + cnt


def _out_proj(x, attn, b, f3, cc, sc, w_fmix, ga, gf, w_out, g2, w_router, b_router):
    batch, n, _ = x.shape
    n2 = DFT_N2
    n1 = n // n2
    kc = DFT3_ROWS
    nj = n1 // kc
    tm = n2 * kc
    assert tm == TOKEN_TILE
    t = batch * n
    row = lambda i: (i, 0)
    grp = lambda i: (i // nj, 0, i % nj, 0)
    n_lane_tiles = FOURIER_WIDTH // LANES
    return pl.pallas_call(
        _outproj_kernel,
        grid=(t // tm,),
        in_specs=[
            pl.BlockSpec((1, n2, kc, D_MODEL), grp),
            pl.BlockSpec((1, n2, kc, ATTN_WIDTH), grp),
            pl.BlockSpec((1, n2, kc, FOURIER_WIDTH), grp),
            _const_spec((2 * n2, 2 * n2)),
            _const_spec((FOURIER_GROUP_DIM, FOURIER_GROUP_DIM)),
            _const_spec((FOURIER_GROUP_DIM, FOURIER_GROUP_DIM)),
            _const_spec((FOURIER_WIDTH, FOURIER_WIDTH)),
            _const_spec((1, ATTN_WIDTH)),
            _const_spec((1, FOURIER_WIDTH)),
            _const_spec((D_MODEL, D_MODEL)),
            _const_spec((1, D_MODEL)),
            _const_spec((D_MODEL, LANES)),
            _const_spec((1, LANES)),
        ],
        out_specs=[
            pl.BlockSpec((tm, D_MODEL), row),
            pl.BlockSpec((tm, D_MODEL // 2), row),
            pl.BlockSpec((tm, LANES), row),
            pl.BlockSpec((8, LANES), lambda i: (0, 0)),
        ],
        out_shape=[
            jax.ShapeDtypeStruct((t, D_MODEL), F32),
            jax.ShapeDtypeStruct((t, D_MODEL // 2), jnp.uint32),
            jax.ShapeDtypeStruct((t, LANES), F32),
            jax.ShapeDtypeStruct((8, LANES), F32),
        ],
        scratch_shapes=[
            pltpu.VMEM((n_lane_tiles, tm, LANES), jnp.uint32),
            pltpu.VMEM((n_lane_tiles, tm, LANES), jnp.uint32),
        ],
        compiler_params=_cparams(("arbitrary",)),
        name="out_proj_router",
    )(x.reshape(batch, n2, n1, D_MODEL), attn.reshape(batch, n2, n1, ATTN_WIDTH), b, f3,
      cc, sc, w_fmix, ga, gf, w_out, g2, w_router, b_router)


def _rank_kernel(info_ref, start_ref, pos_ref, base_ref):
    i = pl.program_id(0)
    tm = info_ref.shape[0]

    @pl.when(i == 0)
    def _():
        base_ref[...] = start_ref[...]

    bucket = info_ref[...].T[0:1, :]
    brow = lax.broadcasted_iota(jnp.int32, (32, tm), 0).astype(F32)
    onehot = jnp.where(brow == bucket, 1.0, 0.0)
    oh = onehot.astype(BF16)
    src = lax.broadcasted_iota(jnp.int32, (tm, tm), 0)
    dst = lax.broadcasted_iota(jnp.int32, (tm, tm), 1)
    upper = jnp.where(src <= dst, 1.0, 0.0).astype(BF16)
    cum = jnp.dot(oh, upper, preferred_element_type=F32)
    base = base_ref[...]
    base_t = jnp.concatenate([base] * (tm // LANES), axis=1)
    pos = jnp.sum(onehot * (cum - 1.0 + base_t), axis=0, keepdims=True)
    pos_ref[0] = pos.astype(jnp.int32)
    ones = jnp.ones((tm, LANES), BF16)
    base_ref[...] = base + jnp.dot(oh, ones, preferred_element_type=F32)


def _rank(info, starts):
    t = info.shape[0]
    tm = TOKEN_TILE
    return pl.pallas_call(
        _rank_kernel,
        grid=(t // tm,),
        in_specs=[
            pl.BlockSpec((tm, LANES), lambda i: (i, 0)),
            pl.BlockSpec((32, LANES), lambda i: (0, 0)),
        ],
        out_specs=pl.BlockSpec((1, 1, tm), lambda i: (i, 0, 0)),
        out_shape=jax.ShapeDtypeStruct((t // tm, 1, tm), jnp.int32),
        scratch_shapes=[pltpu.VMEM((32, LANES), F32)],
        compiler_params=_cparams(("arbitrary",)),
        name="rank",
    )(info, starts)


def _for_each_row(n_rows, fn):
    def body(c, carry):
        for u in range(ROW_DMA_UNROLL):
            fn(c * ROW_DMA_UNROLL + u)
        return carry

    lax.fori_loop(0, n_rows // ROW_DMA_UNROLL, body, 0)


def _scatter_kernel(pad_ref, pos_ref, ha_ref, hb_ref, xs_ref, zero_ref, sem, zsem, *, steps_a, n_steps,
                    n_tiles):
    i = pl.program_id(0)
    tm = ha_ref.shape[0]
    r_tile = zero_ref.shape[0]

    @pl.when(i == 0)
    def _():
        zero = jnp.zeros(zero_ref.shape, F32)
        zero_ref[...] = pltpu.pack_elementwise([zero, zero], packed_dtype=BF16)

        def tile_copy(t):
            return pltpu.make_async_copy(zero_ref, xs_ref.at[pl.ds(pl.multiple_of(t * r_tile, r_tile), r_tile)],
                                         zsem)

        def start_tile(t, carry):
            tile_copy(t).start()
            return carry

        def wait_tile(t, carry):
            tile_copy(t).wait()
            return carry

        for b in range(N_BUCKETS):
            lax.fori_loop(pad_ref[2 * b], pad_ref[2 * b + 1], start_tile, 0)
        lax.fori_loop(pad_ref[2 * N_BUCKETS], n_tiles, start_tile, 0)
        for b in range(N_BUCKETS):
            lax.fori_loop(pad_ref[2 * b], pad_ref[2 * b + 1], wait_tile, 0)
        lax.fori_loop(pad_ref[2 * N_BUCKETS], n_tiles, wait_tile, 0)

    def scatter_from(h_ref):
        def row_copy(r):
            return pltpu.make_async_copy(h_ref.at[pl.ds(r, 1)], xs_ref.at[pl.ds(pos_ref[0, 0, r], 1)], sem)

        _for_each_row(tm, lambda r: row_copy(r).start())
        _for_each_row(tm, lambda r: row_copy(r).wait())

    @pl.when(i < steps_a)
    def _():
        scatter_from(ha_ref)

    @pl.when(i >= steps_a)
    def _():
        scatter_from(hb_ref)


def _scatter(pad, pos, h_a, h_b, n_tiles):
    w = h_a.shape[1]
    tm = TOKEN_TILE
    steps_a = h_a.shape[0] // tm
    n_steps = steps_a + h_b.shape[0] // tm
    r = EXPERT_TILE
    return pl.pallas_call(
        functools.partial(_scatter_kernel, steps_a=steps_a, n_steps=n_steps, n_tiles=n_tiles),
        grid_spec=pltpu.PrefetchScalarGridSpec(
            num_scalar_prefetch=1,
            grid=(n_steps,),
            in_specs=[
                pl.BlockSpec((1, 1, tm), lambda i, pad: (i, 0, 0), memory_space=pltpu.SMEM),
                pl.BlockSpec((tm, w), lambda i, pad: (jnp.minimum(i, steps_a - 1), 0)),
                pl.BlockSpec((tm, w), lambda i, pad: (jnp.maximum(i - steps_a, 0), 0)),
            ],
            out_specs=pl.BlockSpec(memory_space=pl.ANY),
            scratch_shapes=[pltpu.VMEM((r, w), jnp.uint32), pltpu.SemaphoreType.DMA,
                            pltpu.SemaphoreType.DMA],
        ),
        out_shape=jax.ShapeDtypeStruct((n_tiles * r, w), jnp.uint32),
        compiler_params=_cparams(("arbitrary",)),
        name="scatter_rows",
    )(pad, pos, h_a, h_b)


def _expert_kernel(lo_ref, hi_ref, blk_ref, valid_ref, x_ref,
                   wg0_ref, wu0_ref, wd0_ref, wg1_ref, wu1_ref, wd1_ref, o_ref):
    del lo_ref, hi_ref, blk_ref
    i = pl.program_id(0)

    @pl.when(valid_ref[i] == 1)
    def _():
        xw = x_ref[...]
        x = jnp.concatenate(
            [pltpu.unpack_elementwise(xw, index=0, packed_dtype=BF16, unpacked_dtype=F32),
             pltpu.unpack_elementwise(xw, index=1, packed_dtype=BF16, unpacked_dtype=F32)],
            axis=1).astype(BF16)

        def mlp(wg_ref, wu_ref, wd_ref):
            a = jnp.dot(x, wg_ref[0], preferred_element_type=F32)
            u = jnp.dot(x, wu_ref[0], preferred_element_type=F32)
            h = (a * (1.0 / (1.0 + jnp.exp(-a))) * u).astype(BF16)
            return jnp.dot(h, wd_ref[0], preferred_element_type=F32)

        e_lo = mlp(wg0_ref, wu0_ref, wd0_ref)
        e_hi = mlp(wg1_ref, wu1_ref, wd1_ref)
        o_ref[...] = pltpu.pack_elementwise([e_lo, e_hi], packed_dtype=BF16)

    @pl.when(valid_ref[i] == 0)
    def _():
        zero = jnp.zeros(o_ref.shape, F32)
        o_ref[...] = pltpu.pack_elementwise([zero, zero], packed_dtype=BF16)


def _experts(lo, hi, blk, valid, xs, wg, wu, wd):
    n_tiles = lo.shape[0]
    r = EXPERT_TILE
    x_map = lambda i, lo, hi, blk, valid: (blk[i], 0)
    lo_map = lambda i, lo, hi, blk, valid: (lo[i], 0, 0)
    hi_map = lambda i, lo, hi, blk, valid: (hi[i], 0, 0)
    one = pl.Buffered(1)
    w_up = (1, D_MODEL, EXPERT_FF)
    w_dn = (1, EXPERT_FF, D_MODEL)
    return pl.pallas_call(
        _expert_kernel,
        grid_spec=pltpu.PrefetchScalarGridSpec(
            num_scalar_prefetch=4,
            grid=(n_tiles,),
            in_specs=[
                pl.BlockSpec((r, D_MODEL // 2), x_map),
                pl.BlockSpec(w_up, lo_map, pipeline_mode=one),
                pl.BlockSpec(w_up, lo_map, pipeline_mode=one),
                pl.BlockSpec(w_dn, lo_map, pipeline_mode=one),
                pl.BlockSpec(w_up, hi_map),
                pl.BlockSpec(w_up, hi_map),
                pl.BlockSpec(w_dn, hi_map),
            ],
            out_specs=pl.BlockSpec((r, D_MODEL), lambda i, lo, hi, blk, valid: (i, 0)),
        ),
        out_shape=jax.ShapeDtypeStruct((n_tiles * r, D_MODEL), jnp.uint32),
        compiler_params=_cparams(("arbitrary",)),
        name="experts",
    )(lo, hi, blk, valid, xs, wg, wu, wd, wg, wu, wd)


def _combine_kernel(pos_ref, nxt_ref, x1_ref, info_ref, eo_ref, y_ref, gbuf, sems, *, n_steps):
    i = pl.program_id(0)
    tm = x1_ref.shape[0]
    slot = lax.rem(i, 2)

    def row_copy(p_ref, s, r):
        return pltpu.make_async_copy(eo_ref.at[pl.ds(p_ref[0, 0, r], 1)], gbuf.at[s, pl.ds(r, 1)],
                                     sems.at[s])

    @pl.when(i == 0)
    def _():
        _for_each_row(tm, lambda r: row_copy(pos_ref, 0, r).start())

    @pl.when(i + 1 < n_steps)
    def _():
        _for_each_row(tm, lambda r: row_copy(nxt_ref, 1 - slot, r).start())

    _for_each_row(tm, lambda r: row_copy(pos_ref, slot, r).wait())

    g = gbuf[slot]
    e_lo = pltpu.unpack_elementwise(g, index=0, packed_dtype=BF16, unpacked_dtype=F32)
    e_hi = pltpu.unpack_elementwise(g, index=1, packed_dtype=BF16, unpacked_dtype=F32)
    info = info_ref[...]
    y = x1_ref[...] + info[:, 1:2] * e_lo + info[:, 2:3] * e_hi
    y_ref[0] = y.reshape(y_ref.shape[1:])


def _combine(pos, x1, info, eo, batch, n):
    n2 = DFT_N2
    n1 = n // n2
    kc = DFT3_ROWS
    nj = n1 // kc
    tm = n2 * kc
    n_steps = batch * nj
    return pl.pallas_call(
        functools.partial(_combine_kernel, n_steps=n_steps),
        grid=(n_steps,),
        in_specs=[
            pl.BlockSpec((1, 1, tm), lambda i: (i, 0, 0), memory_space=pltpu.SMEM),
            pl.BlockSpec((1, 1, tm), lambda i: (jnp.minimum(i + 1, n_steps - 1), 0, 0),
                         memory_space=pltpu.SMEM),
            pl.BlockSpec((tm, D_MODEL), lambda i: (i, 0)),
            pl.BlockSpec((tm, LANES), lambda i: (i, 0)),
            pl.BlockSpec(memory_space=pl.ANY),
        ],
        out_specs=pl.BlockSpec((1, n2, kc, D_MODEL), lambda i: (i // nj, 0, i % nj, 0)),
        out_shape=jax.ShapeDtypeStruct((batch, n2, n1, D_MODEL), F32),
        scratch_shapes=[pltpu.VMEM((2, tm, D_MODEL), jnp.uint32), pltpu.SemaphoreType.DMA((2,))],
        compiler_params=_cparams(("arbitrary",)),
        name="combine",
    )(pos, pos, x1, info, eo)


def _rope_tables(n):
    rows = n // GRID_W
    row_id = np.repeat(np.arange(rows, dtype=np.float64), GRID_W)
    col_id = np.tile(np.arange(GRID_W, dtype=np.float64), rows)
    inv_freq = np.exp(-math.log(ROPE_THETA) * 2.0 * np.arange(ROPE_FREQS, dtype=np.float64) / (2 * ROPE_FREQS))
    ar = row_id[:, None] * inv_freq
    ac = col_id[:, None] * inv_freq
    cr, sr, ccol, scol = np.cos(ar), np.sin(ar), np.cos(ac), np.sin(ac)
    zero = np.zeros_like(sr)
    c = np.concatenate([cr, cr, ccol, ccol], axis=1)
    s1 = np.concatenate([-sr, zero, -scol, zero], axis=1)
    s2 = np.concatenate([zero, sr, zero, scol], axis=1)
    return jnp.asarray(c, F32), jnp.asarray(s1, F32), jnp.asarray(s2, F32)


def _channel_dft_tables():
    g = FOURIER_GROUP_DIM
    a = 2.0 * np.pi * ((np.arange(g)[:, None] * np.arange(g)[None, :]) % g) / g
    return jnp.asarray(np.cos(a), F32), jnp.asarray(np.sin(a), F32)


_PAIR_LO = np.array([0, 0, 0, 1, 1, 2], np.int32)
_PAIR_HI = np.array([1, 2, 3, 2, 3, 3], np.int32)


def kernel(x_prompt, x_sample, norm1_g, w_in, q_norm_g, k_norm_g, w_fmix, attn_out_g, fourier_out_g,
           w_out, norm2_g, w_router_group, b_router_group, w_router_expert, b_router_expert,
           w_gate, w_up, w_down):
    l = 0
    g1 = norm1_g[l][None, :]
    g2 = norm2_g[l][None, :]
    qg = q_norm_g[l][None, :]
    kg = k_norm_g[l][None, :]
    ga = attn_out_g[l][None, :]
    gf = fourier_out_g[l][None, :]
    w_in_b = w_in[l].astype(BF16)
    w_fmix_b = w_fmix[l].astype(BF16)
    w_out_b = w_out[l].astype(BF16)
    wg_b = w_gate[l].astype(BF16)
    wu_b = w_up[l].astype(BF16)
    wd_b = w_down[l].astype(BF16)
    n_r = N_EXPERT_GROUPS + N_EXPERTS
    w_router = jnp.zeros((D_MODEL, LANES), F32)
    w_router = w_router.at[:, :N_EXPERT_GROUPS].set(w_router_group[l])
    w_router = w_router.at[:, N_EXPERT_GROUPS:n_r].set(w_router_expert[l]).astype(BF16)
    b_router = jnp.zeros((1, LANES), F32)
    b_router = b_router.at[0, :N_EXPERT_GROUPS].set(b_router_group[l])
    b_router = b_router.at[0, N_EXPERT_GROUPS:n_r].set(b_router_expert[l])
    cc, sc = _channel_dft_tables()
    score_bound = _score_bound(qg, kg)

    seqs = []
    for x in (x_prompt, x_sample):
        batch, n, _ = x.shape
        x2d = x.reshape(batch * n, D_MODEL)
        rope_c, rope_s1, rope_s2 = _rope_tables(n)
        q, k, vt, f2 = _in_proj(x2d, n, g1, w_in_b, qg, kg, rope_c, rope_s1, rope_s2)
        attn = _attention(q.reshape(batch, n, ATTN_WIDTH), k.reshape(batch, n, KV_WIDTH), vt, score_bound)
        b, f3 = _dft_stage1(f2, n)
        x1, h2, info, cnt = _out_proj(x, attn, b, f3, cc, sc, w_fmix_b, ga, gf, w_out_b, g2, w_router, b_router)
        seqs.append((x.shape, x1, h2, info, cnt))

    r = EXPERT_TILE
    t_all = sum(s[1].shape[0] for s in seqs)
    n_tiles = t_all // r + N_BUCKETS
    counts = sum(s[4][0, :N_BUCKETS] for s in seqs).astype(jnp.int32)
    tiles_b = (counts + (r - 1)) // r
    tile_end = jnp.cumsum(tiles_b)
    tile_start = tile_end - tiles_b
    total_tiles = tile_end[-1]
    starts = jnp.zeros((32,), F32).at[:N_BUCKETS].set((tile_start * r).astype(F32))
    starts = jnp.broadcast_to(starts[:, None], (32, LANES))
    tile_id = jnp.arange(n_tiles, dtype=jnp.int32)
    blk = jnp.minimum(tile_id, total_tiles - 1)
    tile_bucket = jnp.sum((tile_end[None, :] <= blk[:, None]).astype(jnp.int32), axis=1)
    grp = tile_bucket // N_PAIRS
    pair = tile_bucket % N_PAIRS
    lo = grp * EXPERTS_PER_GROUP + jnp.asarray(_PAIR_LO)[pair]
    hi = grp * EXPERTS_PER_GROUP + jnp.asarray(_PAIR_HI)[pair]
    valid = (tile_id < total_tiles).astype(jnp.int32)

    info_all = jnp.concatenate([s[3] for s in seqs], axis=0)
    pos = _rank(info_all, starts)

    has = (tiles_b > 0).astype(jnp.int32)
    pad = jnp.concatenate([jnp.stack([(tile_end - 1) * has, tile_end * has], axis=1).reshape(-1),
                           total_tiles[None]]).astype(jnp.int32)
    xs = _scatter(pad, pos, seqs[0][2], seqs[1][2], n_tiles)
    eo = _experts(lo, hi, blk, valid, xs, wg_b, wu_b, wd_b)

    outs = []
    off = 0
    for shape, x1, _, info, _ in seqs:
        nt = x1.shape[0] // TOKEN_TILE
        y = _combine(pos[off:off + nt], x1, info, eo, shape[0], shape[1])
        outs.append(y.reshape(shape))
        off += nt
    return tuple(outs)
```
